```python
import math
import jax, jax.numpy as jnp
from jax import lax
import numpy as np

D_MODEL = 4096
BATCH = 4
SEQ = 2048
DEPTH = 4
DEC_BATCH = 32
DEC_SEQ = 8
PAST_LEN = 8192
PAGE_SIZE = 128

N_A_LAYERS = DEPTH // 2
N_B_LAYERS = DEPTH - N_A_LAYERS
HEAD_DIM_A = 128
HEADS_A = D_MODEL // (2 * HEAD_DIM_A)
KV_HEADS_A = 4
GROUP_A = HEADS_A // KV_HEADS_A
HEAD_DIM_B = 64
HEADS_B = D_MODEL // HEAD_DIM_B
KV_HEADS_B = 8
GROUP_B = HEADS_B // KV_HEADS_B
WINDOW = 128
Q_BLOCK = 128
ROPE_THETA = 500000.0
ROPE_FRACTION = 4
D_FF = 7 * D_MODEL // 2
N_EXPERTS = 8
TOP_K = 2
D_FF_EXPERT = D_FF // 2
N_DENSE = (DEPTH + 1) // 2
N_MOE = DEPTH // 2
NORM_EPS = 1e-5

kernel_name = 'yoco_diffattn_swa_sink_moe_step'


def _lambda_init(layer):
    return 0.8 - 0.6 * math.exp(-0.3 * layer)


def _rms(x, g):
    xf = x.astype(jnp.float32)
    y = xf * lax.rsqrt(jnp.mean(xf * xf, axis=-1, keepdims=True) + NORM_EPS)
    return (y * g.astype(jnp.float32)).astype(x.dtype)


def _rope(x, pos):
    d = x.shape[-1]
    r = d // ROPE_FRACTION
    half = r // 2
    inv = ROPE_THETA ** (-(2.0 / r) * jnp.arange(half, dtype=jnp.float32))
    ang = pos.astype(jnp.float32)[:, None] * inv[None, :]
    cos = jnp.cos(ang)[:, None, :]
    sin = jnp.sin(ang)[:, None, :]
    xr = x[..., :r].astype(jnp.float32)
    x1, x2 = xr[..., :half], xr[..., half:]
    rot = jnp.concatenate([x1 * cos - x2 * sin, x2 * cos + x1 * sin], axis=-1).astype(x.dtype)
    return jnp.concatenate([rot, x[..., r:]], axis=-1)


def _diff_qkv(x, pos, g_norm, w_qkv, g_q, g_k):
    b, t, _ = x.shape
    qkv = _rms(x, g_norm) @ w_qkv
    nq = HEADS_A * 2 * HEAD_DIM_A
    nk = KV_HEADS_A * 2 * HEAD_DIM_A
    q = qkv[..., :nq].reshape(b, t, 2 * HEADS_A, HEAD_DIM_A)
    k = qkv[..., nq:nq + nk].reshape(b, t, 2 * KV_HEADS_A, HEAD_DIM_A)
    v = qkv[..., nq + nk:].reshape(b, t, KV_HEADS_A, 2 * HEAD_DIM_A)
    q = _rope(_rms(q, g_q), pos).reshape(b, t, KV_HEADS_A, GROUP_A, 2, HEAD_DIM_A)
    k = _rope(_rms(k, g_k), pos).reshape(b, t, KV_HEADS_A, 2, HEAD_DIM_A)
    return q, k, v


def _diff_core(q, ks, vs, masks, lam, g_sub, lam_init):
    scale = q.shape[-1] ** -0.5
    s = jnp.concatenate(
        [jnp.where(m, jnp.einsum('btngcd,bsncd->bngcts', q, k, preferred_element_type=jnp.float32), -jnp.inf)
         for k, m in zip(ks, masks)], axis=-1) * scale
    p = jax.nn.softmax(s, axis=-1)
    a = p[:, :, :, 0] - lam * p[:, :, :, 1]
    o = None
    start = 0
    for v in vs:
        stop = start + v.shape[1]
        part = jnp.einsum('bngts,bsne->btnge', a[..., start:stop].astype(v.dtype), v)
        o = part if o is None else o + part
        start = stop
    o = _rms(o, g_sub) * (1.0 - lam_init)
    return o.reshape(o.shape[0], o.shape[1], -1)


def _diff_attn_prompt(q, k, v, lam, g_sub, lam_init):
    b, s = q.shape[:2]
    nb = s // Q_BLOCK
    qb = q.reshape(b, nb, Q_BLOCK, *q.shape[2:]).swapaxes(0, 1)
    kpos = jnp.arange(s)

    def block(args):
        qi, i = args
        qpos = i * Q_BLOCK + jnp.arange(Q_BLOCK)
        mask = kpos[None, :] <= qpos[:, None]
        return _diff_core(qi, (k,), (v,), (mask,), lam, g_sub, lam_init)

    out = lax.map(block, (qb, jnp.arange(nb)))
    return out.swapaxes(0, 1).reshape(b, s, -1)


def _shared_kv(x, pos, g_norm, w_kv, g_k):
    b, t, _ = x.shape
    kv = _rms(x, g_norm) @ w_kv
    nk = KV_HEADS_B * HEAD_DIM_B
    k = kv[..., :nk].reshape(b, t, KV_HEADS_B, HEAD_DIM_B)
    v = kv[..., nk:].reshape(b, t, KV_HEADS_B, HEAD_DIM_B)
    return _rope(_rms(k, g_k), pos), v


def _swa_q(x, pos, g_norm, w_q, g_q):
    b, t, _ = x.shape
    q = (_rms(x, g_norm) @ w_q).reshape(b, t, HEADS_B, HEAD_DIM_B)
    return _rope(_rms(q, g_q), pos).reshape(b, t, KV_HEADS_B, GROUP_B, HEAD_DIM_B)


def _band(x):
    b, s = x.shape[:2]
    xb = x.reshape(b, s // WINDOW, WINDOW, *x.shape[2:])
    prev = jnp.pad(xb, ((0, 0), (1, 0), (0, 0), (0, 0), (0, 0)))[:, :-1]
    return jnp.concatenate([prev, xb], axis=2)


def _band_mask(n_blocks):
    c = jnp.arange(n_blocks)[:, None, None]
    i = jnp.arange(WINDOW)[None, :, None]
    j = jnp.arange(2 * WINDOW)[None, None, :]
    qpos = c * WINDOW + i
    kpos = (c - 1) * WINDOW + j
    m = (kpos <= qpos) & (qpos - kpos < WINDOW) & (kpos >= 0)
    return m[:, None, None]


def _sink_core(q, k, v, mask, sinks):
    n, g = q.shape[-3], q.shape[-2]
    s = jnp.einsum('...tngd,...snd->...ngts', q, k, preferred_element_type=jnp.float32) * (q.shape[-1] ** -0.5)
    s = jnp.where(mask, s, -jnp.inf)
    sink = jnp.broadcast_to(sinks.astype(jnp.float32).reshape(n, g, 1, 1), s.shape[:-1] + (1,))
    p = jax.nn.softmax(jnp.concatenate([s, sink], axis=-1), axis=-1)[..., :-1]
    return jnp.einsum('...ngts,...snd->...tngd', p.astype(v.dtype), v)


def _swiglu(h, w_gu, w_down):
    gu = h @ w_gu
    g, u = jnp.split(gu, 2, axis=-1)
    return (jax.nn.silu(g) * u) @ w_down


def _moe(h, router, w_gu, w_down):
    logits = jnp.einsum('btd,de->bte', h, router, preferred_element_type=jnp.float32)
    top_v, top_i = lax.top_k(logits, TOP_K)
    gates = jax.nn.softmax(top_v, axis=-1)
    w = jnp.sum(jax.nn.one_hot(top_i, N_EXPERTS, dtype=jnp.float32) * gates[..., None], axis=-2)
    out = jnp.zeros_like(h)
    for e in range(N_EXPERTS):
        out = out + w[..., e:e + 1].astype(h.dtype) * _swiglu(h, w_gu[e], w_down[e])
    return out


def setup_inputs(seed: int = 0) -> dict:
    key = jax.random.key(seed)
    keys = jax.random.split(key, 32)

    def nrm(i, shape, scale):
        return jax.random.normal(keys[i], shape, jnp.float32) * scale

    def gain(i, shape):
        return 1.0 + nrm(i, shape, 0.02)

    d = D_MODEL
    n_pages = PAST_LEN // PAGE_SIZE
    n_pool = (5 * DEC_BATCH * n_pages + 3) // 4
    w_buf = min(WINDOW, PAST_LEN)
    qkv_a = (HEADS_A + 2 * KV_HEADS_A) * 2 * HEAD_DIM_A
    o_a = HEADS_A * 2 * HEAD_DIM_A
    q_b = HEADS_B * HEAD_DIM_B
    page_table = jax.random.permutation(keys[6], n_pool)[:DEC_BATCH * n_pages].reshape(DEC_BATCH, n_pages).astype(jnp.int32)
    return {
        'x_prompt': nrm(0, (BATCH, SEQ, d), 1.0),
        'x_sample': nrm(1, (DEC_BATCH, DEC_SEQ, d), 1.0),
        'cache_k': nrm(2, (N_A_LAYERS, n_pool, PAGE_SIZE, KV_HEADS_A, 2 * HEAD_DIM_A), 1.0),
        'cache_v': nrm(3, (N_A_LAYERS, n_pool, PAGE_SIZE, KV_HEADS_A, 2 * HEAD_DIM_A), 1.0),
        'cache_swa_k': nrm(4, (DEC_BATCH, w_buf, KV_HEADS_B, HEAD_DIM_B), 1.0),
        'cache_swa_v': nrm(5, (DEC_BATCH, w_buf, KV_HEADS_B, HEAD_DIM_B), 1.0),
        'page_table': page_table,
        'a_norm': gain(7, (N_A_LAYERS, d)),
        'a_wqkv': nrm(8, (N_A_LAYERS, d, qkv_a), d ** -0.5),
        'a_qn': gain(9, (N_A_LAYERS, HEAD_DIM_A)),
        'a_kn': gain(10, (N_A_LAYERS, HEAD_DIM_A)),
        'a_lq1': nrm(11, (N_A_LAYERS, HEAD_DIM_A), 0.1),
        'a_lk1': nrm(12, (N_A_LAYERS, HEAD_DIM_A), 0.1),
        'a_lq2': nrm(13, (N_A_LAYERS, HEAD_DIM_A), 0.1),
        'a_lk2': nrm(14, (N_A_LAYERS, HEAD_DIM_A), 0.1),
        'a_subln': gain(15, (N_A_LAYERS, 2 * HEAD_DIM_A)),
        'a_wo': nrm(16, (N_A_LAYERS, o_a, d), o_a ** -0.5),
        'kv_norm': gain(17, (d,)),
        'kv_w': nrm(18, (d, 2 * KV_HEADS_B * HEAD_DIM_B), d ** -0.5),
        'kv_kn': gain(19, (HEAD_DIM_B,)),
        'b_norm': gain(20, (N_B_LAYERS, d)),
        'b_wq': nrm(21, (N_B_LAYERS, d, q_b), d ** -0.5),
        'b_qn': gain(22, (N_B_LAYERS, HEAD_DIM_B)),
        'b_sinks': nrm(23, (N_B_LAYERS, HEADS_B), 0.5),
        'b_wo': nrm(24, (N_B_LAYERS, q_b, d), q_b ** -0.5),
        'f_norm': gain(25, (DEPTH, d)),
        'd_wgu': nrm(26, (N_DENSE, d, 2 * D_FF), d ** -0.5),
        'd_wdown': nrm(27, (N_DENSE, D_FF, d), D_FF ** -0.5),
        'm_router': nrm(28, (N_MOE, d, N_EXPERTS), d ** -0.5),
        'm_wgu': nrm(29, (N_MOE, N_EXPERTS, d, 2 * D_FF_EXPERT), d ** -0.5),
        'm_wdown': nrm(30, (N_MOE, N_EXPERTS, D_FF_EXPERT, d), D_FF_EXPERT ** -0.5),
    }


def reference(x_prompt, x_sample, cache_k, cache_v, cache_swa_k, cache_swa_v, page_table,
              a_norm, a_wqkv, a_qn, a_kn, a_lq1, a_lk1, a_lq2, a_lk2, a_subln, a_wo,
              kv_norm, kv_w, kv_kn, b_norm, b_wq, b_qn, b_sinks, b_wo,
              f_norm, d_wgu, d_wdown, m_router, m_wgu, m_wdown):
    seq = x_prompt.shape[1]
    dec_b, dec_t = x_sample.shape[:2]
    past_len = page_table.shape[1] * cache_k.shape[2]
    pos_p = jnp.arange(seq)
    pos_s = past_len + jnp.arange(dec_t)
    mask_past = jnp.ones((dec_t, past_len), dtype=bool)
    mask_new = jnp.tril(jnp.ones((dec_t, dec_t), dtype=bool))
    xp, xs = x_prompt, x_sample
    kp_rows, vp_rows, ks_rows, vs_rows = [], [], [], []
    for l in range(DEPTH):
        if l < N_A_LAYERS:
            lam_init = _lambda_init(l)
            lam = (jnp.exp(jnp.sum(a_lq1[l].astype(jnp.float32) * a_lk1[l].astype(jnp.float32)))
                   - jnp.exp(jnp.sum(a_lq2[l].astype(jnp.float32) * a_lk2[l].astype(jnp.float32)))
                   + lam_init)
            qp, kp, vp = _diff_qkv(xp, pos_p, a_norm[l], a_wqkv[l], a_qn[l], a_kn[l])
            qs, ks, vs = _diff_qkv(xs, pos_s, a_norm[l], a_wqkv[l], a_qn[l], a_kn[l])
            k_past = cache_k[l, page_table].reshape(dec_b, past_len, KV_HEADS_A, 2, HEAD_DIM_A)
            v_past = cache_v[l, page_table].reshape(dec_b, past_len, KV_HEADS_A, 2 * HEAD_DIM_A)
            op = _diff_attn_prompt(qp, kp, vp, lam, a_subln[l], lam_init)
            os_ = _diff_core(qs, (k_past, ks), (v_past, vs), (mask_past, mask_new), lam, a_subln[l], lam_init)
            xp = xp + op @ a_wo[l]
            xs = xs + os_ @ a_wo[l]
            kp_rows.append(kp.reshape(kp.shape[0], kp.shape[1], KV_HEADS_A, 2 * HEAD_DIM_A))
            vp_rows.append(vp)
            ks_rows.append(ks.reshape(ks.shape[0], ks.shape[1], KV_HEADS_A, 2 * HEAD_DIM_A))
            vs_rows.append(vs)
        else:
            if l == N_A_LAYERS:
                k_sh_p, v_sh_p = _shared_kv(xp, pos_p, kv_norm, kv_w, kv_kn)
                k_sh_s, v_sh_s = _shared_kv(xs, pos_s, kv_norm, kv_w, kv_kn)
                w_buf = cache_swa_k.shape[1]
                k_win = jnp.concatenate([cache_swa_k, k_sh_s], axis=1)
                v_win = jnp.concatenate([cache_swa_v, v_sh_s], axis=1)
                kpos_win = jnp.concatenate([past_len - w_buf + jnp.arange(w_buf), pos_s])
                mask_win = (kpos_win[None, :] <= pos_s[:, None]) & (pos_s[:, None] - kpos_win[None, :] < WINDOW)
                nb = seq // WINDOW
                k_band, v_band = _band(k_sh_p), _band(v_sh_p)
                mask_band = _band_mask(nb)
                w_keep = min(WINDOW, seq)
                swa_k_p, swa_v_p = k_sh_p[:, seq - w_keep:], v_sh_p[:, seq - w_keep:]
                swa_k_s, swa_v_s = k_win[:, dec_t:], v_win[:, dec_t:]
            j = l - N_A_LAYERS
            qp = _swa_q(xp, pos_p, b_norm[j], b_wq[j], b_qn[j])
            qs = _swa_q(xs, pos_s, b_norm[j], b_wq[j], b_qn[j])
            qpb = qp.reshape(qp.shape[0], nb, WINDOW, KV_HEADS_B, GROUP_B, HEAD_DIM_B)
            op = _sink_core(qpb, k_band, v_band, mask_band, b_sinks[j]).reshape(qp.shape[0], seq, -1)
            os_ = _sink_core(qs, k_win, v_win, mask_win, b_sinks[j]).reshape(dec_b, dec_t, -1)
            xp = xp + op @ b_wo[j]
            xs = xs + os_ @ b_wo[j]
        hp = _rms(xp, f_norm[l])
        hs = _rms(xs, f_norm[l])
        i = l // 2
        if l % 2 == 0:
            xp = xp + _swiglu(hp, d_wgu[i], d_wdown[i])
            xs = xs + _swiglu(hs, d_wgu[i], d_wdown[i])
        else:
            xp = xp + _moe(hp, m_router[i], m_wgu[i], m_wdown[i])
            xs = xs + _moe(hs, m_router[i], m_wgu[i], m_wdown[i])
    k_prompt = jnp.stack(kp_rows)
    v_prompt = jnp.stack(vp_rows)
    k_sample = jnp.stack(ks_rows)
    v_sample = jnp.stack(vs_rows)
    return (xp, xs, k_prompt, v_prompt, k_sample, v_sample, swa_k_p, swa_v_p, swa_k_s, swa_v_s)
```

```python
import functools
import math

import jax
import jax.numpy as jnp
from jax import lax
from jax.experimental import pallas as pl
from jax.experimental.pallas import tpu as pltpu

HEAD_DIM_A = 128
KV_HEADS_A = 4
HEAD_DIM_B = 64
KV_HEADS_B = 8
WINDOW = 128
ROPE_THETA = 500000.0
ROPE_FRACTION = 4
N_TOP = 2
NORM_EPS = 1e-5
LANES = 128
NEG = -1e30
VMEM_LIMIT = 56 * 1024 * 1024
MOE_ROW_TILE = 1152

BF16 = jnp.bfloat16
F32 = jnp.float32


def _lambda_init(layer):
    return 0.8 - 0.6 * math.exp(-0.3 * layer)


def _tile(n, pref, mult=8):
    best = None
    for t in range(mult, min(n, pref) + 1, mult):
        if n % t == 0:
            best = t
    return best if best is not None else n


def _params(*sem):
    return pltpu.CompilerParams(dimension_semantics=sem, vmem_limit_bytes=VMEM_LIMIT)


def _rmsnorm_kernel(x_ref, g_ref, o_ref):
    x = x_ref[...]
    ms = jnp.mean(x * x, axis=-1, keepdims=True)
    o_ref[...] = (x * lax.rsqrt(ms + NORM_EPS) * g_ref[...]).astype(o_ref.dtype)


def _rmsnorm(x, g, out_dtype):
    t, d = x.shape
    tm = _tile(t, 256)
    return pl.pallas_call(
        _rmsnorm_kernel,
        grid=(t // tm,),
        in_specs=[pl.BlockSpec((tm, d), lambda i: (i, 0)),
                  pl.BlockSpec((1, d), lambda i: (0, 0))],
        out_specs=pl.BlockSpec((tm, d), lambda i: (i, 0)),
        out_shape=jax.ShapeDtypeStruct((t, d), out_dtype),
        compiler_params=_params("parallel"),
        name="rmsnorm",
    )(x, g.reshape(1, d))


def _router_kernel(n_experts, x_ref, g_ref, r_ref, h_ref, meta_ref, cnt_ref, carry_ref):
    i = pl.program_id(0)

    @pl.when(i == 0)
    def _():
        carry_ref[...] = jnp.zeros_like(carry_ref)

    x = x_ref[...]
    ms = jnp.mean(x * x, axis=-1, keepdims=True)
    h = x * lax.rsqrt(ms + NORM_EPS) * g_ref[...]
    h_ref[...] = h
    tm = x.shape[0]
    logits = jnp.dot(h, r_ref[...], preferred_element_type=F32, precision=lax.Precision.HIGHEST)
    lane = lax.broadcasted_iota(jnp.int32, (tm, LANES), 1)
    logits = jnp.where(lane < n_experts, logits, NEG)
    m1 = jnp.max(logits, axis=-1, keepdims=True)
    i1 = jnp.min(jnp.where(logits == m1, lane, LANES), axis=-1, keepdims=True)
    rest = jnp.where(lane == i1, NEG, logits)
    m2 = jnp.max(rest, axis=-1, keepdims=True)
    i2 = jnp.min(jnp.where(rest == m2, lane, LANES), axis=-1, keepdims=True)
    e2 = jnp.exp(m2 - m1)
    g1 = 1.0 / (1.0 + e2)
    g2 = e2 / (1.0 + e2)
    sel = jnp.where((lane == i1) | (lane == i2), 1.0, 0.0)
    row = lax.broadcasted_iota(jnp.int32, (tm, tm), 0)
    col = lax.broadcasted_iota(jnp.int32, (tm, tm), 1)
    lower = jnp.where(col < row, 1.0, 0.0).astype(BF16)
    before = jnp.dot(lower, sel.astype(BF16), preferred_element_type=F32)
    rank = carry_ref[...] + before
    r1 = jnp.sum(jnp.where(lane == i1, rank, 0.0), axis=-1, keepdims=True)
    r2 = jnp.sum(jnp.where(lane == i2, rank, 0.0), axis=-1, keepdims=True)
    meta = jnp.where(lane == 0, i1.astype(F32), 0.0)
    meta = jnp.where(lane == 1, i2.astype(F32), meta)
    meta = jnp.where(lane == 2, g1, meta)
    meta = jnp.where(lane == 3, g2, meta)
    meta = jnp.where(lane == 4, r1, meta)
    meta = jnp.where(lane == 5, r2, meta)
    meta_ref[...] = meta
    carry_ref[...] = carry_ref[...] + jnp.sum(sel, axis=0, keepdims=True)
    cnt_ref[...] = carry_ref[...]


def _rmsnorm_router(x, g, router):
    t, d = x.shape
    n_experts = router.shape[1]
    tm = _tile(t, 256)
    r_pad = jnp.pad(router, ((0, 0), (0, LANES - n_experts)))
    return pl.pallas_call(
        functools.partial(_router_kernel, n_experts),
        grid=(t // tm,),
        in_specs=[pl.BlockSpec((tm, d), lambda i: (i, 0)),
                  pl.BlockSpec((1, d), lambda i: (0, 0)),
                  pl.BlockSpec((d, LANES), lambda i: (0, 0))],
        out_specs=[pl.BlockSpec((tm, d), lambda i: (i, 0)),
                   pl.BlockSpec((tm, LANES), lambda i: (i, 0)),
                   pl.BlockSpec((1, LANES), lambda i: (0, 0))],
        out_shape=[jax.ShapeDtypeStruct((t, d), F32),
                   jax.ShapeDtypeStruct((t, LANES), F32),
                   jax.ShapeDtypeStruct((1, LANES), F32)],
        scratch_shapes=[pltpu.VMEM((1, LANES), F32)],
        compiler_params=_params("arbitrary"),
        name="rmsnorm_router",
    )(x, g.reshape(1, d), r_pad)


def _mm_kernel(x_ref, w_ref, o_ref):
    o_ref[...] = jnp.dot(x_ref[...], w_ref[...].astype(BF16), preferred_element_type=F32)


def _mm_res_kernel(x_ref, w_ref, r_ref, o_ref):
    o_ref[...] = r_ref[...] + jnp.dot(x_ref[...], w_ref[...].astype(BF16), preferred_element_type=F32)


def _matmul(x, w, residual=None):
    m, k = x.shape
    n = w.shape[1]
    tm = _tile(m, 1408)
    tn = _tile(n, 256, LANES)
    in_specs = [pl.BlockSpec((tm, k), lambda i, j: (i, 0)),
                pl.BlockSpec((k, tn), lambda i, j: (0, j))]
    args = [x, w]
    kern = _mm_kernel
    if residual is not None:
        in_specs.append(pl.BlockSpec((tm, tn), lambda i, j: (i, j)))
        args.append(residual)
        kern = _mm_res_kernel
    return pl.pallas_call(
        kern,
        grid=(m // tm, n // tn),
        in_specs=in_specs,
        out_specs=pl.BlockSpec((tm, tn), lambda i, j: (i, j)),
        out_shape=jax.ShapeDtypeStruct((m, n), F32),
        compiler_params=_params("parallel", "parallel"),
        name="matmul",
    )(*args)


def _gu_kernel(te_ref, tv_ref, x_ref, wg_ref, wu_ref, o_ref):
    m = pl.program_id(0)

    @pl.when(tv_ref[m] == 1)
    def _():
        x = x_ref[...]
        g = jnp.dot(x, wg_ref[...].astype(BF16), preferred_element_type=F32)
        u = jnp.dot(x, wu_ref[...].astype(BF16), preferred_element_type=F32)
        o_ref[...] = (g * jax.nn.sigmoid(g) * u).astype(o_ref.dtype)

    @pl.when(tv_ref[m] == 0)
    def _():
        o_ref[...] = jnp.zeros_like(o_ref)


def _gu_matmul(x, w_gu, te, tv, tm):
    m, k = x.shape
    f = w_gu.shape[2] // 2
    tn = _tile(f, 256, LANES)
    nt = f // tn

    def col(j, tv_ref, i):
        return jnp.where(tv_ref[i] == 1, j, nt - 1)

    grid_spec = pltpu.PrefetchScalarGridSpec(
        num_scalar_prefetch=2,
        grid=(m // tm, nt),
        in_specs=[pl.BlockSpec((tm, k), lambda i, j, te, tv: (i, 0)),
                  pl.BlockSpec((None, k, tn), lambda i, j, te, tv: (te[i], 0, col(j, tv, i))),
                  pl.BlockSpec((None, k, tn), lambda i, j, te, tv: (te[i], 0, nt + col(j, tv, i)))],
        out_specs=pl.BlockSpec((tm, tn), lambda i, j, te, tv: (i, j)),
    )
    return pl.pallas_call(
        _gu_kernel,
        grid_spec=grid_spec,
        out_shape=jax.ShapeDtypeStruct((m, f), BF16),
        compiler_params=_params("parallel", "arbitrary"),
        name="swiglu_gate_up",
    )(te, tv, x, w_gu, w_gu)


def _down_kernel(has_res, te_ref, tv_ref, a_ref, w_ref, *rest):
    if has_res:
        r_ref, o_ref, acc_ref = rest
    else:
        o_ref, acc_ref = rest
    m = pl.program_id(0)
    kk = pl.program_id(2)
    last = pl.num_programs(2) - 1

    @pl.when(kk == 0)
    def _():
        acc_ref[...] = jnp.zeros_like(acc_ref)

    @pl.when(tv_ref[m] == 1)
    def _():
        acc_ref[...] += jnp.dot(a_ref[...], w_ref[...].astype(BF16), preferred_element_type=F32)

    @pl.when(kk == last)
    def _():
        if has_res:
            o_ref[...] = r_ref[...] + acc_ref[...]
        else:
            o_ref[...] = acc_ref[...]


def _down_matmul(a, w_down, te, tv, tm, residual=None):
    m, f = a.shape
    n = w_down.shape[2]
    tn = _tile(n, 1024, LANES)
    tk = _tile(f, 1024, LANES)
    nk = f // tk

    def kblk(kk, tv_ref, i):
        return jnp.where(tv_ref[i] == 1, kk, nk - 1)

    in_specs = [pl.BlockSpec((tm, tk), lambda i, j, kk, te, tv: (i, kblk(kk, tv, i))),
                pl.BlockSpec((None, tk, tn), lambda i, j, kk, te, tv: (te[i], kblk(kk, tv, i), j))]
    args = [a, w_down]
    if residual is not None:
        in_specs.append(pl.BlockSpec((tm, tn), lambda i, j, kk, te, tv: (i, j)))
        args.append(residual)
    grid_spec = pltpu.PrefetchScalarGridSpec(
        num_scalar_prefetch=2,
        grid=(m // tm, n // tn, nk),
        in_specs=in_specs,
        out_specs=pl.BlockSpec((tm, tn), lambda i, j, kk, te, tv: (i, j)),
        scratch_shapes=[pltpu.VMEM((tm, tn), F32)],
    )
    return pl.pallas_call(
        functools.partial(_down_kernel, residual is not None),
        grid_spec=grid_spec,
        out_shape=jax.ShapeDtypeStruct((m, n), F32),
        compiler_params=_params("parallel", "parallel", "arbitrary"),
        name="swiglu_down",
    )(te, tv, *args)


def _prep_kernel(hd, half, scale, x_ref, g_ref, c_ref, s1_ref, s2_ref, o_ref):
    w = x_ref.shape[1]
    g = g_ref[...]
    c = c_ref[...]
    s1 = s1_ref[...]
    s2 = s2_ref[...]
    lane = lax.broadcasted_iota(jnp.int32, (x_ref.shape[0], LANES), 1)
    for s in range(w // LANES):
        x = x_ref[:, s * LANES:(s + 1) * LANES]
        sq = x * x
        if hd == LANES:
            ms = jnp.mean(sq, axis=-1, keepdims=True)
        else:
            lo = lane < hd
            s_lo = jnp.sum(jnp.where(lo, sq, 0.0), axis=-1, keepdims=True)
            s_hi = jnp.sum(jnp.where(lo, 0.0, sq), axis=-1, keepdims=True)
            ms = jnp.where(lo, s_lo, s_hi) / hd
        y = x * lax.rsqrt(ms + NORM_EPS) * g
        y = y * c + pltpu.roll(y, half, 1) * s1 + pltpu.roll(y, LANES - half, 1) * s2
        if scale != 1.0:
            y = y * scale
        o_ref[:, s * LANES:(s + 1) * LANES] = y.astype(o_ref.dtype)


def _head_prep(x, col_block, width, gain, tables, hd, scale, out_dtype):
    t = x.shape[0]
    tm = _tile(t, 256)
    half = hd // ROPE_FRACTION // 2
    g = jnp.tile(gain, LANES // hd).reshape(1, LANES)
    tab_spec = pl.BlockSpec((tm, LANES), lambda i: (i, 0))
    return pl.pallas_call(
        functools.partial(_prep_kernel, hd, half, scale),
        grid=(t // tm,),
        in_specs=[pl.BlockSpec((tm, width), lambda i: (i, col_block)),
                  pl.BlockSpec((1, LANES), lambda i: (0, 0)),
                  tab_spec, tab_spec, tab_spec],
        out_specs=pl.BlockSpec((tm, width), lambda i: (i, 0)),
        out_shape=jax.ShapeDtypeStruct((t, width), out_dtype),
        compiler_params=_params("parallel"),
        name="head_norm_rope",
    )(x, g, *tables)


def _rope_tables(pos, hd):
    r = hd // ROPE_FRACTION
    half = r // 2
    inv = ROPE_THETA ** (-(2.0 / r) * jnp.arange(half, dtype=F32))
    ang = pos.astype(F32)[:, None] * inv[None, :]
    lane = jnp.arange(LANES) % hd
    cos = jnp.cos(ang)[:, lane % half]
    sin = jnp.sin(ang)[:, lane % half]
    c = jnp.where(lane < r, cos, 1.0)
    s1 = jnp.where((lane >= half) & (lane < r), sin, 0.0)
    s2 = jnp.where(lane < half, -sin, 0.0)
    return c, s1, s2


def _lambda_value(lq1_ref, lk1_ref, lq2_ref, lk2_ref, lam_init):
    a = jnp.sum(lq1_ref[...] * lk1_ref[...], axis=-1, keepdims=True)
    b = jnp.sum(lq2_ref[...] * lk2_ref[...], axis=-1, keepdims=True)
    return jnp.exp(a) - jnp.exp(b) + lam_init


def _sub_norm(o, gs_ref, lam_init):
    ms = jnp.mean(o * o, axis=-1, keepdims=True)
    return o * lax.rsqrt(ms + NORM_EPS) * gs_ref[...] * (1.0 - lam_init)


def _diff_prompt_kernel(lam_init, group, q_ref, k_ref, v_ref, lq1_ref, lk1_ref, lq2_ref, lk2_ref,
                        gs_ref, o_ref, m_ref, l_ref, acc_ref):
    i = pl.program_id(2)
    tq = q_ref.shape[0]
    d = HEAD_DIM_A
    rows = group * tq
    qpos = i * tq + lax.broadcasted_iota(jnp.int32, (rows, tq), 0) % tq
    kcol = lax.broadcasted_iota(jnp.int32, (rows, tq), 1)
    outs = []
    for c in range(2):
        qc = jnp.concatenate(
            [q_ref[:, (g * 2 + c) * d:(g * 2 + c + 1) * d] for g in range(group)], axis=0)
        m_ref[...] = jnp.full_like(m_ref, NEG)
        l_ref[...] = jnp.zeros_like(l_ref)
        acc_ref[...] = jnp.zeros_like(acc_ref)

        def body(j, carry, qc=qc, c=c):
            start = pl.multiple_of(j * tq, tq)
            kj = k_ref[pl.ds(start, tq), c * d:(c + 1) * d].astype(BF16)
            vj = v_ref[pl.ds(start, tq), :].astype(BF16)
            s = lax.dot_general(qc, kj, (((1,), (1,)), ((), ())), preferred_element_type=F32)
            s = jnp.where(j * tq + kcol <= qpos, s, NEG)
            m_old = m_ref[...]
            m_new = jnp.maximum(m_old, jnp.max(s, axis=-1, keepdims=True))
            alpha = jnp.exp(m_old - m_new)
            p = jnp.exp(s - m_new)
            l_ref[...] = alpha * l_ref[...] + jnp.sum(p, axis=-1, keepdims=True)
            acc_ref[...] = alpha * acc_ref[...] + jnp.dot(p.astype(BF16), vj, preferred_element_type=F32)
            m_ref[...] = m_new
            return carry

        lax.fori_loop(0, i + 1, body, 0)
        outs.append(acc_ref[...] / l_ref[...])
    lam = _lambda_value(lq1_ref, lk1_ref, lq2_ref, lk2_ref, lam_init)
    o = _sub_norm(outs[0] - lam * outs[1], gs_ref, lam_init)
    for g in range(group):
        o_ref[:, g * 2 * d:(g + 1) * 2 * d] = o[g * tq:(g + 1) * tq].astype(o_ref.dtype)


def _diff_attn_prompt(q, k, qkv, v_col_block, batch, seq, lam_params, g_sub, lam_init):
    d2 = 2 * HEAD_DIM_A
    hw = q.shape[1]
    group = hw // (KV_HEADS_A * d2)
    tq = _tile(seq, 256)
    nqb = seq // tq
    rows = group * tq
    vec = pl.BlockSpec((1, HEAD_DIM_A), lambda b, n, i: (0, 0))
    return pl.pallas_call(
        functools.partial(_diff_prompt_kernel, lam_init, group),
        grid=(batch, KV_HEADS_A, nqb),
        in_specs=[pl.BlockSpec((tq, group * d2), lambda b, n, i: (b * nqb + i, n)),
                  pl.BlockSpec((seq, d2), lambda b, n, i: (b, n)),
                  pl.BlockSpec((seq, d2), lambda b, n, i: (b, v_col_block + n)),
                  vec, vec, vec, vec,
                  pl.BlockSpec((1, d2), lambda b, n, i: (0, 0))],
        out_specs=pl.BlockSpec((tq, group * d2), lambda b, n, i: (b * nqb + i, n)),
        out_shape=jax.ShapeDtypeStruct((batch * seq, hw), BF16),
        scratch_shapes=[pltpu.VMEM((rows, 1), F32), pltpu.VMEM((rows, 1), F32),
                        pltpu.VMEM((rows, d2), F32)],
        compiler_params=_params("parallel", "parallel", "arbitrary"),
        name="diff_attn_prompt",
    )(q, k, qkv, *lam_params, g_sub.reshape(1, d2))


def _diff_decode_kernel(lam_init, group, pt_ref, q_ref, kc_ref, vc_ref, kn_ref, vn_ref,
                        lq1_ref, lk1_ref, lq2_ref, lk2_ref, gs_ref, o_ref,
                        qb_ref, m_ref, l_ref, acc_ref):
    del pt_ref
    p = pl.program_id(1)
    last = pl.num_programs(1) - 1
    t = q_ref.shape[0]
    d = HEAD_DIM_A
    rows = 2 * group * t

    @pl.when(p == 0)
    def _():
        qb_ref[...] = jnp.zeros_like(qb_ref)
        for n in range(KV_HEADS_A):
            for c in range(2):
                for g in range(group):
                    r0 = n * rows + (c * group + g) * t
                    col = ((n * group + g) * 2 + c) * d
                    qb_ref[r0:r0 + t, c * d:(c + 1) * d] = q_ref[:, col:col + d].astype(BF16)
        m_ref[...] = jnp.full_like(m_ref, NEG)
        l_ref[...] = jnp.zeros_like(l_ref)
        acc_ref[...] = jnp.zeros_like(acc_ref)

    def update(n, kn, vn, mask):
        sl = slice(n * rows, (n + 1) * rows)
        s = lax.dot_general(qb_ref[sl, :], kn.astype(BF16), (((1,), (1,)), ((), ())),
                            preferred_element_type=F32)
        if mask is not None:
            s = jnp.where(mask, s, NEG)
        m_old = m_ref[sl, :]
        m_new = jnp.maximum(m_old, jnp.max(s, axis=-1, keepdims=True))
        alpha = jnp.exp(m_old - m_new)
        pr = jnp.exp(s - m_new)
        l_ref[sl, :] = alpha * l_ref[sl, :] + jnp.sum(pr, axis=-1, keepdims=True)
        acc_ref[sl, :] = alpha * acc_ref[sl, :] + jnp.dot(pr.astype(BF16), vn.astype(BF16),
                                                          preferred_element_type=F32)
        m_ref[sl, :] = m_new

    for n in range(KV_HEADS_A):
        update(n, kc_ref[:, n, :], vc_ref[:, n, :], None)

    @pl.when(p == last)
    def _():
        nk = kn_ref.shape[0]
        tok = lax.broadcasted_iota(jnp.int32, (rows, nk), 0) % t
        key = lax.broadcasted_iota(jnp.int32, (rows, nk), 1)
        lam = _lambda_value(lq1_ref, lk1_ref, lq2_ref, lk2_ref, lam_init)
        for n in range(KV_HEADS_A):
            update(n, kn_ref[:, n * 2 * d:(n + 1) * 2 * d], vn_ref[:, n * 2 * d:(n + 1) * 2 * d],
                   key <= tok)
            sl = slice(n * rows, (n + 1) * rows)
            o2 = acc_ref[sl, :] / l_ref[sl, :]
            half = group * t
            o = _sub_norm(o2[:half] - lam * o2[half:], gs_ref, lam_init)
            for g in range(group):
                col = (n * group + g) * 2 * d
                o_ref[:, col:col + 2 * d] = o[g * t:(g + 1) * t]


def _diff_attn_decode(q_s, cache_k, cache_v, layer, page_table, k_new, v_new, lam_params, g_sub,
                      lam_init):
    nb, t, hw = q_s.shape
    n_pages = page_table.shape[1]
    page = cache_k.shape[2]
    d2 = 2 * HEAD_DIM_A
    group = hw // (KV_HEADS_A * d2)
    rows = KV_HEADS_A * 2 * group * t
    vec = pl.BlockSpec((1, HEAD_DIM_A), lambda b, p, pt: (0, 0))
    cache_spec = pl.BlockSpec((None, None, page, KV_HEADS_A, d2),
                              lambda b, p, pt: (layer, pt[b * n_pages + p], 0, 0, 0))
    new_spec = pl.BlockSpec((None, page, KV_HEADS_A * d2), lambda b, p, pt: (b, 0, 0))
    grid_spec = pltpu.PrefetchScalarGridSpec(
        num_scalar_prefetch=1,
        grid=(nb, n_pages),
        in_specs=[pl.BlockSpec((None, t, hw), lambda b, p, pt: (b, 0, 0)),
                  cache_spec, cache_spec, new_spec, new_spec,
                  vec, vec, vec, vec,
                  pl.BlockSpec((1, d2), lambda b, p, pt: (0, 0))],
        out_specs=pl.BlockSpec((None, t, hw), lambda b, p, pt: (b, 0, 0)),
        scratch_shapes=[pltpu.VMEM((rows, d2), BF16), pltpu.VMEM((rows, 1), F32),
                        pltpu.VMEM((rows, 1), F32), pltpu.VMEM((rows, d2), F32)],
    )
    return pl.pallas_call(
        functools.partial(_diff_decode_kernel, lam_init, group),
        grid_spec=grid_spec,
        out_shape=jax.ShapeDtypeStruct((nb, t, hw), F32),
        compiler_params=_params("parallel", "arbitrary"),
        name="diff_attn_decode",
    )(page_table.reshape(-1), q_s, cache_k, cache_v, k_new, v_new, *lam_params,
      g_sub.reshape(1, d2))


def _swa_kernel(group, blocks_per_seq, q_ref, kp_ref, kc_ref, vp_ref, vc_ref, sink_ref, o_ref):
    tq = q_ref.shape[0]
    d = HEAD_DIM_B
    rows = group * tq
    tok = lax.broadcasted_iota(jnp.int32, (rows, WINDOW), 0) % tq
    key = lax.broadcasted_iota(jnp.int32, (rows, WINDOW), 1)
    cur_ok = key <= tok
    if blocks_per_seq is None:
        prev_ok = key > tok
    else:
        first = pl.program_id(0) % blocks_per_seq == 0
        prev_ok = key > tok + jnp.where(first, WINDOW, 0)
    dims = (((1,), (1,)), ((), ()))
    for n in range(KV_HEADS_B):
        cs = slice(n * d, (n + 1) * d)
        qn = jnp.concatenate(
            [q_ref[:, (n * group + g) * d:(n * group + g + 1) * d].astype(BF16) for g in range(group)],
            axis=0)
        sink = jnp.concatenate(
            [jnp.full((tq, 1), sink_ref[n * group + g], F32) for g in range(group)], axis=0)
        sp = lax.dot_general(qn, kp_ref[:, cs].astype(BF16), dims, preferred_element_type=F32)
        sc = lax.dot_general(qn, kc_ref[:, cs].astype(BF16), dims, preferred_element_type=F32)
        sp = jnp.where(prev_ok, sp, NEG)
        sc = jnp.where(cur_ok, sc, NEG)
        m = jnp.maximum(jnp.maximum(jnp.max(sp, axis=-1, keepdims=True),
                                    jnp.max(sc, axis=-1, keepdims=True)), sink)
        pp = jnp.exp(sp - m)
        pc = jnp.exp(sc - m)
        den = (jnp.sum(pp, axis=-1, keepdims=True) + jnp.sum(pc, axis=-1, keepdims=True)
               + jnp.exp(sink - m))
        o = (jnp.dot(pp.astype(BF16), vp_ref[:, cs].astype(BF16), preferred_element_type=F32)
             + jnp.dot(pc.astype(BF16), vc_ref[:, cs].astype(BF16), preferred_element_type=F32)) / den
        for g in range(group):
            col = (n * group + g) * d
            o_ref[:, col:col + d] = o[g * tq:(g + 1) * tq].astype(o_ref.dtype)


def _swa_prompt(q, k_sh, kv, sinks, batch, seq):
    hw = q.shape[1]
    kw = KV_HEADS_B * HEAD_DIM_B
    group = hw // kw
    nb = seq // WINDOW
    qspec = pl.BlockSpec((WINDOW, hw), lambda r: (r, 0))
    return pl.pallas_call(
        functools.partial(_swa_kernel, group, nb),
        grid=(batch * nb,),
        in_specs=[qspec,
                  pl.BlockSpec((WINDOW, kw), lambda r: (jnp.maximum(r - 1, 0), 0)),
                  pl.BlockSpec((WINDOW, kw), lambda r: (r, 0)),
                  pl.BlockSpec((WINDOW, kw), lambda r: (jnp.maximum(r - 1, 0), 1)),
                  pl.BlockSpec((WINDOW, kw), lambda r: (r, 1)),
                  pl.BlockSpec(memory_space=pltpu.SMEM)],
        out_specs=qspec,
        out_shape=jax.ShapeDtypeStruct((batch * seq, hw), BF16),
        compiler_params=_params("parallel"),
        name="swa_attn_prompt",
    )(q, k_sh, k_sh, kv, kv, sinks)


def _swa_sample(q_s, k_prev, k_cur, v_prev, v_cur, sinks):
    nb, t, hw = q_s.shape
    kw = KV_HEADS_B * HEAD_DIM_B
    group = hw // kw
    qspec = pl.BlockSpec((None, t, hw), lambda b: (b, 0, 0))
    kspec = pl.BlockSpec((None, WINDOW, kw), lambda b: (b, 0, 0))
    return pl.pallas_call(
        functools.partial(_swa_kernel, group, None),
        grid=(nb,),
        in_specs=[qspec, kspec, kspec, kspec, kspec, pl.BlockSpec(memory_space=pltpu.SMEM)],
        out_specs=qspec,
        out_shape=jax.ShapeDtypeStruct((nb, t, hw), F32),
        compiler_params=_params("parallel"),
        name="swa_attn_sample",
    )(q_s, k_prev, k_cur, v_prev, v_cur, sinks)


def _row_copy(src_hbm, dst_vmem, sem, src_row, dst_row):
    return pltpu.make_async_copy(src_hbm.at[pl.ds(src_row, 1)], dst_vmem.at[pl.ds(dst_row, 1)], sem)


def _dispatch_kernel(src_ref, cv_ref, h_hbm, o_ref, buf_ref, sem):
    i = pl.program_id(0)
    tr = buf_ref.shape[0]

    @pl.when(cv_ref[i] == 1)
    def _():
        def start(r, carry):
            _row_copy(h_hbm, buf_ref, sem, src_ref[i * tr + r], r).start()
            return carry

        def wait(r, carry):
            _row_copy(h_hbm, buf_ref, sem, 0, r).wait()
            return carry

        lax.fori_loop(0, tr, start, 0)
        lax.fori_loop(0, tr, wait, 0)
        o_ref[...] = buf_ref[...].astype(o_ref.dtype)

    @pl.when(cv_ref[i] == 0)
    def _():
        o_ref[...] = jnp.zeros_like(o_ref)


def _dispatch(h, src, chunk_valid, tr):
    d = h.shape[1]
    p_rows = src.shape[0]
    grid_spec = pltpu.PrefetchScalarGridSpec(
        num_scalar_prefetch=2,
        grid=(p_rows // tr,),
        in_specs=[pl.BlockSpec(memory_space=pl.ANY)],
        out_specs=pl.BlockSpec((tr, d), lambda i, src, cv: (i, 0)),
        scratch_shapes=[pltpu.VMEM((tr, d), F32), pltpu.SemaphoreType.DMA(())],
    )
    return pl.pallas_call(
        _dispatch_kernel,
        grid_spec=grid_spec,
        out_shape=jax.ShapeDtypeStruct((p_rows, d), BF16),
        compiler_params=_params("arbitrary"),
        name="moe_dispatch",
    )(src, chunk_valid, h)


def _combine_kernel(pos_ref, x_ref, meta_ref, eo_hbm, o_ref, b0_ref, b1_ref, sem):
    i = pl.program_id(0)
    tr = x_ref.shape[0]

    def start(r, carry):
        t = (i * tr + r) * N_TOP
        _row_copy(eo_hbm, b0_ref, sem, pos_ref[t], r).start()
        _row_copy(eo_hbm, b1_ref, sem, pos_ref[t + 1], r).start()
        return carry

    def wait(r, carry):
        _row_copy(eo_hbm, b0_ref, sem, 0, r).wait()
        _row_copy(eo_hbm, b1_ref, sem, 0, r).wait()
        return carry

    lax.fori_loop(0, tr, start, 0)
    lax.fori_loop(0, tr, wait, 0)
    meta = meta_ref[...]
    o_ref[...] = x_ref[...] + meta[:, 2:3] * b0_ref[...] + meta[:, 3:4] * b1_ref[...]


def _combine(x, meta, eo, pos):
    t, d = x.shape
    tr = _tile(t, 256)
    grid_spec = pltpu.PrefetchScalarGridSpec(
        num_scalar_prefetch=1,
        grid=(t // tr,),
        in_specs=[pl.BlockSpec((tr, d), lambda i, pos: (i, 0)),
                  pl.BlockSpec((tr, LANES), lambda i, pos: (i, 0)),
                  pl.BlockSpec(memory_space=pl.ANY)],
        out_specs=pl.BlockSpec((tr, d), lambda i, pos: (i, 0)),
        scratch_shapes=[pltpu.VMEM((tr, d), F32), pltpu.VMEM((tr, d), F32),
                        pltpu.SemaphoreType.DMA(())],
    )
    return pl.pallas_call(
        _combine_kernel,
        grid_spec=grid_spec,
        out_shape=jax.ShapeDtypeStruct((t, d), F32),
        compiler_params=_params("arbitrary"),
        name="moe_combine",
    )(pos, x, meta, eo)


def _moe(x, g_norm, router, w_gu, w_down):
    t, d = x.shape
    n_experts = router.shape[1]
    tm = MOE_ROW_TILE if N_TOP * t >= n_experts * MOE_ROW_TILE else 32
    n_tiles = -(-(N_TOP * t + n_experts * (tm - 1)) // tm)
    p_rows = n_tiles * tm
    h, meta, counts = _rmsnorm_router(x, g_norm, router)

    cnt = counts[0, :n_experts].astype(jnp.int32)
    tiles_e = (cnt + tm - 1) // tm
    tiles_end = jnp.cumsum(tiles_e)
    row_start = (tiles_end - tiles_e) * tm
    ids = meta[:, 0:2].astype(jnp.int32)
    pos = (row_start[ids] + meta[:, 4:6].astype(jnp.int32)).reshape(-1)
    tok = jnp.repeat(jnp.arange(t, dtype=jnp.int32), N_TOP)
    src = jnp.zeros((p_rows,), jnp.int32).at[pos].set(tok)
    tile_id = jnp.arange(n_tiles, dtype=jnp.int32)
    n_used = tiles_end[-1]
    tv = (tile_id < n_used).astype(jnp.int32)
    te = jnp.searchsorted(tiles_end, jnp.minimum(tile_id, n_used - 1), side="right").astype(jnp.int32)
    te = jnp.minimum(te, n_experts - 1)
    tr = _tile(tm, 256)
    chunk_valid = jnp.repeat(tv, tm // tr)

    hs = _dispatch(h, src, chunk_valid, tr)
    act = _gu_matmul(hs, w_gu, te, tv, tm)
    eo = _down_matmul(act, w_down, te, tv, tm)
    return _combine(x, meta, eo, pos)


def _dense_ffn(x, g_norm, w_gu, w_down):
    t = x.shape[0]
    tm = _tile(t, 1408)
    ones = jnp.ones((t // tm,), jnp.int32)
    zeros = jnp.zeros((t // tm,), jnp.int32)
    h = _rmsnorm(x, g_norm, BF16)
    act = _gu_matmul(h, w_gu[None], zeros, ones, tm)
    return _down_matmul(act, w_down[None], zeros, ones, tm, residual=x)


def kernel(x_prompt, x_sample, cache_k, cache_v, cache_swa_k, cache_swa_v, page_table,
           a_norm, a_wqkv, a_qn, a_kn, a_lq1, a_lk1, a_lq2, a_lk2, a_subln, a_wo,
           kv_norm, kv_w, kv_kn, b_norm, b_wq, b_qn, b_sinks, b_wo,
           f_norm, d_wgu, d_wdown, m_router, m_wgu, m_wdown):
    batch, seq, d_model = x_prompt.shape
    dec_b, dec_t, _ = x_sample.shape
    depth = f_norm.shape[0]
    n_a = a_norm.shape[0]
    page = cache_k.shape[2]
    past_len = page_table.shape[1] * page
    t_p = batch * seq
    t_s = dec_b * dec_t
    d2 = 2 * HEAD_DIM_A
    q_a = d_model
    kv_a = KV_HEADS_A * d2
    kw_b = KV_HEADS_B * HEAD_DIM_B
    assert cache_swa_k.shape[1] == WINDOW and seq % WINDOW == 0 and dec_t <= WINDOW <= page

    x = jnp.concatenate([x_prompt.reshape(t_p, d_model), x_sample.reshape(t_s, d_model)], axis=0)
    pos = jnp.concatenate([jnp.tile(jnp.arange(seq), batch),
                           jnp.tile(past_len + jnp.arange(dec_t), dec_b)])
    tab_a = _rope_tables(pos, HEAD_DIM_A)
    tab_b = _rope_tables(pos, HEAD_DIM_B)

    def pad_rows(a, rows):
        return jnp.pad(a, ((0, 0), (0, rows - a.shape[1]), (0, 0)))

    kp_rows, vp_rows, ks_rows, vs_rows = [], [], [], []
    for l in range(depth):
        if l < n_a:
            lam_init = _lambda_init(l)
            lam_params = [v[l].reshape(1, HEAD_DIM_A) for v in (a_lq1, a_lk1, a_lq2, a_lk2)]
            qkv = _matmul(_rmsnorm(x, a_norm[l], BF16), a_wqkv[l])
            q = _head_prep(qkv, 0, q_a, a_qn[l], tab_a, HEAD_DIM_A, HEAD_DIM_A ** -0.5, BF16)
            k = _head_prep(qkv, q_a // kv_a, kv_a, a_kn[l], tab_a, HEAD_DIM_A, 1.0, F32)
            v = qkv[:, q_a + kv_a:]
            o_p = _diff_attn_prompt(q, k, qkv, (q_a + kv_a) // d2, batch, seq, lam_params,
                                    a_subln[l], lam_init)
            k_s = k[t_p:].reshape(dec_b, dec_t, kv_a)
            v_s = v[t_p:].reshape(dec_b, dec_t, kv_a)
            o_s = _diff_attn_decode(q[t_p:].astype(F32).reshape(dec_b, dec_t, q_a), cache_k, cache_v,
                                    l, page_table, pad_rows(k_s, page), pad_rows(v_s, page),
                                    lam_params, a_subln[l], lam_init)
            o = jnp.concatenate([o_p, o_s.reshape(t_s, q_a).astype(BF16)], axis=0)
            x = _matmul(o, a_wo[l], residual=x)
            kp_rows.append(k[:t_p].reshape(batch, seq, KV_HEADS_A, d2))
            vp_rows.append(v[:t_p].reshape(batch, seq, KV_HEADS_A, d2))
            ks_rows.append(k_s.reshape(dec_b, dec_t, KV_HEADS_A, d2))
            vs_rows.append(v_s.reshape(dec_b, dec_t, KV_HEADS_A, d2))
        else:
            j = l - n_a
            if l == n_a:
                kv = _matmul(_rmsnorm(x, kv_norm, BF16), kv_w)
                k_sh = _head_prep(kv, 0, kw_b, kv_kn, tab_b, HEAD_DIM_B, 1.0, F32)
                v_sh = kv[:, kw_b:]
                k_new = k_sh[t_p:].reshape(dec_b, dec_t, kw_b)
                v_new = v_sh[t_p:].reshape(dec_b, dec_t, kw_b)
                k_prev = cache_swa_k.reshape(dec_b, WINDOW, kw_b)
                v_prev = cache_swa_v.reshape(dec_b, WINDOW, kw_b)
            q = _head_prep(_matmul(_rmsnorm(x, b_norm[j], BF16), b_wq[j]), 0, d_model, b_qn[j],
                           tab_b, HEAD_DIM_B, HEAD_DIM_B ** -0.5, BF16)
            o_p = _swa_prompt(q, k_sh, kv, b_sinks[j], batch, seq)
            o_s = _swa_sample(q[t_p:].astype(F32).reshape(dec_b, dec_t, d_model), k_prev,
                              pad_rows(k_new, WINDOW), v_prev, pad_rows(v_new, WINDOW), b_sinks[j])
            o = jnp.concatenate([o_p, o_s.reshape(t_s, d_model).astype(BF16)], axis=0)
            x = _matmul(o, b_wo[j], residual=x)
        i = l // 2
        if l % 2 == 0:
            x = _dense_ffn(x, f_norm[l], d_wgu[i], d_wdown[i])
        else:
            x = _moe(x, f_norm[l], m_router[i], m_wgu[i], m_wdown[i])

    w_keep = min(WINDOW, seq)
    k_sh_p = k_sh[:t_p].reshape(batch, seq, KV_HEADS_B, HEAD_DIM_B)
    v_sh_p = v_sh[:t_p].reshape(batch, seq, KV_HEADS_B, HEAD_DIM_B)
    swa_k_s = jnp.concatenate([cache_swa_k, k_new.reshape(dec_b, dec_t, KV_HEADS_B, HEAD_DIM_B)],
                              axis=1)[:, dec_t:]
    swa_v_s = jnp.concatenate([cache_swa_v, v_new.reshape(dec_b, dec_t, KV_HEADS_B, HEAD_DIM_B)],
                              axis=1)[:, dec_t:]
    return (x[:t_p].reshape(batch, seq, d_model), x[t_p:].reshape(dec_b, dec_t, d_model),
            jnp.stack(kp_rows), jnp.stack(vp_rows), jnp.stack(ks_rows), jnp.stack(vs_rows),
            k_sh_p[:, seq - w_keep:], v_sh_p[:, seq - w_keep:], swa_k_s, swa_v_s)
```

```python
import functools
import math

import jax
import jax.numpy as jnp
from jax import lax
from jax.experimental import pallas as pl
from jax.experimental.pallas import tpu as pltpu

HEAD_DIM_A = 128
KV_HEADS_A = 4
HEAD_DIM_B = 64
KV_HEADS_B = 8
WINDOW = 128
ROPE_THETA = 500000.0
ROPE_FRACTION = 4
N_TOP = 2
NORM_EPS = 1e-5
LANES = 128
NEG = -1e30
VMEM_LIMIT = 56 * 1024 * 1024
MOE_ROW_TILE = 2304

BF16 = jnp.bfloat16
F32 = jnp.float32


def _lambda_init(layer):
    return 0.8 - 0.6 * math.exp(-0.3 * layer)


def _tile(n, pref, mult=8):
    best = None
    for t in range(mult, min(n, pref) + 1, mult):
        if n % t == 0:
            best = t
    return best if best is not None else n


def _params(*sem):
    return pltpu.CompilerParams(dimension_semantics=sem, vmem_limit_bytes=VMEM_LIMIT)


def _rmsnorm_kernel(x_ref, g_ref, o_ref):
    x = x_ref[...]
    ms = jnp.mean(x * x, axis=-1, keepdims=True)
    o_ref[...] = (x * lax.rsqrt(ms + NORM_EPS) * g_ref[...]).astype(o_ref.dtype)


def _rmsnorm(x, g, out_dtype):
    t, d = x.shape
    tm = _tile(t, 256)
    return pl.pallas_call(
        _rmsnorm_kernel,
        grid=(t // tm,),
        in_specs=[pl.BlockSpec((tm, d), lambda i: (i, 0)),
                  pl.BlockSpec((1, d), lambda i: (0, 0))],
        out_specs=pl.BlockSpec((tm, d), lambda i: (i, 0)),
        out_shape=jax.ShapeDtypeStruct((t, d), out_dtype),
        compiler_params=_params("parallel"),
        name="rmsnorm",
    )(x, g.reshape(1, d))


def _router_kernel(n_experts, x_ref, g_ref, r_ref, h_ref, meta_ref, cnt_ref, carry_ref):
    i = pl.program_id(0)

    @pl.when(i == 0)
    def _():
        carry_ref[...] = jnp.zeros_like(carry_ref)

    x = x_ref[...]
    ms = jnp.mean(x * x, axis=-1, keepdims=True)
    h = x * lax.rsqrt(ms + NORM_EPS) * g_ref[...]
    h_ref[...] = h
    tm = x.shape[0]
    logits = jnp.dot(h, r_ref[...], preferred_element_type=F32, precision=lax.Precision.HIGHEST)
    lane = lax.broadcasted_iota(jnp.int32, (tm, LANES), 1)
    logits = jnp.where(lane < n_experts, logits, NEG)
    m1 = jnp.max(logits, axis=-1, keepdims=True)
    i1 = jnp.min(jnp.where(logits == m1, lane, LANES), axis=-1, keepdims=True)
    rest = jnp.where(lane == i1, NEG, logits)
    m2 = jnp.max(rest, axis=-1, keepdims=True)
    i2 = jnp.min(jnp.where(rest == m2, lane, LANES), axis=-1, keepdims=True)
    e2 = jnp.exp(m2 - m1)
    g1 = 1.0 / (1.0 + e2)
    g2 = e2 / (1.0 + e2)
    sel = jnp.where((lane == i1) | (lane == i2), 1.0, 0.0)
    row = lax.broadcasted_iota(jnp.int32, (tm, tm), 0)
    col = lax.broadcasted_iota(jnp.int32, (tm, tm), 1)
    lower = jnp.where(col < row, 1.0, 0.0).astype(BF16)
    before = jnp.dot(lower, sel.astype(BF16), preferred_element_type=F32)
    rank = carry_ref[...] + before
    r1 = jnp.sum(jnp.where(lane == i1, rank, 0.0), axis=-1, keepdims=True)
    r2 = jnp.sum(jnp.where(lane == i2, rank, 0.0), axis=-1, keepdims=True)
    meta = jnp.where(lane == 0, i1.astype(F32), 0.0)
    meta = jnp.where(lane == 1, i2.astype(F32), meta)
    meta = jnp.where(lane == 2, g1, meta)
    meta = jnp.where(lane == 3, g2, meta)
    meta = jnp.where(lane == 4, r1, meta)
    meta = jnp.where(lane == 5, r2, meta)
    meta_ref[...] = meta
    carry_ref[...] = carry_ref[...] + jnp.sum(sel, axis=0, keepdims=True)
    cnt_ref[...] = carry_ref[...]


def _rmsnorm_router(x, g, router):
    t, d = x.shape
    n_experts = router.shape[1]
    tm = _tile(t, 256)
    r_pad = jnp.pad(router, ((0, 0), (0, LANES - n_experts)))
    return pl.pallas_call(
        functools.partial(_router_kernel, n_experts),
        grid=(t // tm,),
        in_specs=[pl.BlockSpec((tm, d), lambda i: (i, 0)),
                  pl.BlockSpec((1, d), lambda i: (0, 0)),
                  pl.BlockSpec((d, LANES), lambda i: (0, 0))],
        out_specs=[pl.BlockSpec((tm, d), lambda i: (i, 0)),
                   pl.BlockSpec((tm, LANES), lambda i: (i, 0)),
                   pl.BlockSpec((1, LANES), lambda i: (0, 0))],
        out_shape=[jax.ShapeDtypeStruct((t, d), F32),
                   jax.ShapeDtypeStruct((t, LANES), F32),
                   jax.ShapeDtypeStruct((1, LANES), F32)],
        scratch_shapes=[pltpu.VMEM((1, LANES), F32)],
        compiler_params=_params("arbitrary"),
        name="rmsnorm_router",
    )(x, g.reshape(1, d), r_pad)


def _mm_kernel(x_ref, w_ref, o_ref):
    o_ref[...] = jnp.dot(x_ref[...], w_ref[...].astype(BF16), preferred_element_type=F32)


def _mm_res_kernel(x_ref, w_ref, r_ref, o_ref):
    o_ref[...] = r_ref[...] + jnp.dot(x_ref[...], w_ref[...].astype(BF16), preferred_element_type=F32)


def _matmul(x, w, layer, residual=None):
    m, k = x.shape
    n = w.shape[2]
    tm = _tile(m, 1408)
    tn = _tile(n, 256, LANES)
    in_specs = [pl.BlockSpec((tm, k), lambda i, j: (i, 0)),
                pl.BlockSpec((None, k, tn), lambda i, j: (layer, 0, j))]
    args = [x, w]
    kern = _mm_kernel
    if residual is not None:
        in_specs.append(pl.BlockSpec((tm, tn), lambda i, j: (i, j)))
        args.append(residual)
        kern = _mm_res_kernel
    return pl.pallas_call(
        kern,
        grid=(m // tm, n // tn),
        in_specs=in_specs,
        out_specs=pl.BlockSpec((tm, tn), lambda i, j: (i, j)),
        out_shape=jax.ShapeDtypeStruct((m, n), F32),
        compiler_params=_params("parallel", "parallel"),
        name="matmul",
    )(*args)


def _gu_kernel(te_ref, tv_ref, x_ref, wg_ref, wu_ref, o_ref):
    m = pl.program_id(0)

    @pl.when(tv_ref[m] == 1)
    def _():
        x = x_ref[...]
        g = jnp.dot(x, wg_ref[...].astype(BF16), preferred_element_type=F32)
        u = jnp.dot(x, wu_ref[...].astype(BF16), preferred_element_type=F32)
        o_ref[...] = (g * jax.nn.sigmoid(g) * u).astype(o_ref.dtype)

    @pl.when(tv_ref[m] == 0)
    def _():
        o_ref[...] = jnp.zeros_like(o_ref)


def _gu_matmul(x, w_gu, layer, te, tv, tm):
    m, k = x.shape
    f = w_gu.shape[3] // 2
    tn = _tile(f, 256, LANES)
    nt = f // tn

    def col(j, tv_ref, i):
        return jnp.where(tv_ref[i] == 1, j, nt - 1)

    x_mode = {"pipeline_mode": pl.Buffered(1)} if tm * k * 2 > 12 * 1024 * 1024 else {}
    grid_spec = pltpu.PrefetchScalarGridSpec(
        num_scalar_prefetch=2,
        grid=(m // tm, nt),
        in_specs=[pl.BlockSpec((tm, k), lambda i, j, te, tv: (i, 0), **x_mode),
                  pl.BlockSpec((None, None, k, tn),
                               lambda i, j, te, tv: (layer, te[i], 0, col(j, tv, i))),
                  pl.BlockSpec((None, None, k, tn),
                               lambda i, j, te, tv: (layer, te[i], 0, nt + col(j, tv, i)))],
        out_specs=pl.BlockSpec((tm, tn), lambda i, j, te, tv: (i, j)),
    )
    return pl.pallas_call(
        _gu_kernel,
        grid_spec=grid_spec,
        out_shape=jax.ShapeDtypeStruct((m, f), BF16),
        compiler_params=_params("parallel", "arbitrary"),
        name="swiglu_gate_up",
    )(te, tv, x, w_gu, w_gu)


def _down_kernel(has_res, te_ref, tv_ref, a_ref, w_ref, *rest):
    if has_res:
        r_ref, o_ref, acc_ref = rest
    else:
        o_ref, acc_ref = rest
    m = pl.program_id(0)
    kk = pl.program_id(2)
    last = pl.num_programs(2) - 1

    @pl.when(kk == 0)
    def _():
        acc_ref[...] = jnp.zeros_like(acc_ref)

    @pl.when(tv_ref[m] == 1)
    def _():
        acc_ref[...] += jnp.dot(a_ref[...], w_ref[...].astype(BF16), preferred_element_type=F32)

    @pl.when(kk == last)
    def _():
        if has_res:
            o_ref[...] = r_ref[...] + acc_ref[...]
        else:
            o_ref[...] = acc_ref[...]


def _down_matmul(a, w_down, layer, te, tv, tm, residual=None):
    m, f = a.shape
    n = w_down.shape[3]
    tn = _tile(n, 1024, LANES)
    tk = _tile(f, 1024, LANES)
    nk = f // tk

    def kblk(kk, tv_ref, i):
        return jnp.where(tv_ref[i] == 1, kk, nk - 1)

    in_specs = [pl.BlockSpec((tm, tk), lambda i, j, kk, te, tv: (i, kblk(kk, tv, i))),
                pl.BlockSpec((None, None, tk, tn),
                             lambda i, j, kk, te, tv: (layer, te[i], kblk(kk, tv, i), j))]
    args = [a, w_down]
    if residual is not None:
        in_specs.append(pl.BlockSpec((tm, tn), lambda i, j, kk, te, tv: (i, j)))
        args.append(residual)
    grid_spec = pltpu.PrefetchScalarGridSpec(
        num_scalar_prefetch=2,
        grid=(m // tm, n // tn, nk),
        in_specs=in_specs,
        out_specs=pl.BlockSpec((tm, tn), lambda i, j, kk, te, tv: (i, j)),
        scratch_shapes=[pltpu.VMEM((tm, tn), F32)],
    )
    return pl.pallas_call(
        functools.partial(_down_kernel, residual is not None),
        grid_spec=grid_spec,
        out_shape=jax.ShapeDtypeStruct((m, n), F32),
        compiler_params=_params("parallel", "parallel", "arbitrary"),
        name="swiglu_down",
    )(te, tv, *args)


def _prep_kernel(hd, half, scale, x_ref, g_ref, c_ref, s1_ref, s2_ref, o_ref):
    w = x_ref.shape[1]
    g = g_ref[...]
    c = c_ref[...]
    s1 = s1_ref[...]
    s2 = s2_ref[...]
    lane = lax.broadcasted_iota(jnp.int32, (x_ref.shape[0], LANES), 1)
    for s in range(w // LANES):
        x = x_ref[:, s * LANES:(s + 1) * LANES]
        sq = x * x
        if hd == LANES:
            ms = jnp.mean(sq, axis=-1, keepdims=True)
        else:
            lo = lane < hd
            s_lo = jnp.sum(jnp.where(lo, sq, 0.0), axis=-1, keepdims=True)
            s_hi = jnp.sum(jnp.where(lo, 0.0, sq), axis=-1, keepdims=True)
            ms = jnp.where(lo, s_lo, s_hi) / hd
        y = x * lax.rsqrt(ms + NORM_EPS) * g
        y = y * c + pltpu.roll(y, half, 1) * s1 + pltpu.roll(y, LANES - half, 1) * s2
        if scale != 1.0:
            y = y * scale
        o_ref[:, s * LANES:(s + 1) * LANES] = y.astype(o_ref.dtype)


def _head_prep(x, col_block, width, gain, tables, hd, scale, out_dtype):
    t = x.shape[0]
    tm = _tile(t, 256)
    half = hd // ROPE_FRACTION // 2
    g = jnp.tile(gain, LANES // hd).reshape(1, LANES)
    tab_spec = pl.BlockSpec((tm, LANES), lambda i: (i, 0))
    return pl.pallas_call(
        functools.partial(_prep_kernel, hd, half, scale),
        grid=(t // tm,),
        in_specs=[pl.BlockSpec((tm, width), lambda i: (i, col_block)),
                  pl.BlockSpec((1, LANES), lambda i: (0, 0)),
                  tab_spec, tab_spec, tab_spec],
        out_specs=pl.BlockSpec((tm, width), lambda i: (i, 0)),
        out_shape=jax.ShapeDtypeStruct((t, width), out_dtype),
        compiler_params=_params("parallel"),
        name="head_norm_rope",
    )(x, g, *tables)


def _rope_tables(pos, hd):
    r = hd // ROPE_FRACTION
    half = r // 2
    inv = ROPE_THETA ** (-(2.0 / r) * jnp.arange(half, dtype=F32))
    ang = pos.astype(F32)[:, None] * inv[None, :]
    lane = jnp.arange(LANES) % hd
    cos = jnp.cos(ang)[:, lane % half]
    sin = jnp.sin(ang)[:, lane % half]
    c = jnp.where(lane < r, cos, 1.0)
    s1 = jnp.where((lane >= half) & (lane < r), sin, 0.0)
    s2 = jnp.where(lane < half, -sin, 0.0)
    return c, s1, s2


def _lambda_value(lq1_ref, lk1_ref, lq2_ref, lk2_ref, lam_init):
    a = jnp.sum(lq1_ref[...] * lk1_ref[...], axis=-1, keepdims=True)
    b = jnp.sum(lq2_ref[...] * lk2_ref[...], axis=-1, keepdims=True)
    return jnp.exp(a) - jnp.exp(b) + lam_init


def _sub_norm(o, gs_ref, lam_init):
    ms = jnp.mean(o * o, axis=-1, keepdims=True)
    return o * lax.rsqrt(ms + NORM_EPS) * gs_ref[...] * (1.0 - lam_init)


def _diff_prompt_kernel(lam_init, group, q_ref, k_ref, v_ref, lq1_ref, lk1_ref, lq2_ref, lk2_ref,
                        gs_ref, o_ref, s_ref, mx_ref, ls_ref, acc_ref):
    i = pl.program_id(2)
    tq = q_ref.shape[0]
    d = HEAD_DIM_A
    rows = group * tq
    dims = (((1,), (1,)), ((), ()))
    tok = lax.broadcasted_iota(jnp.int32, (rows, tq), 0) % tq
    key = lax.broadcasted_iota(jnp.int32, (rows, tq), 1)

    def fold(x, op):
        out = x[:, :LANES]
        for t in range(1, tq // LANES):
            out = op(out, x[:, t * LANES:(t + 1) * LANES])
        return out

    outs = []
    for c in range(2):
        qc = jnp.concatenate(
            [q_ref[:, (g * 2 + c) * d:(g * 2 + c + 1) * d] for g in range(group)], axis=0)

        def k_block(j):
            start = pl.multiple_of(j * tq, tq)
            return k_ref[pl.ds(start, tq), c * d:(c + 1) * d].astype(BF16)

        mx_ref[...] = jnp.full_like(mx_ref, NEG)

        def scores(j, carry):
            s = lax.dot_general(qc, k_block(j), dims, preferred_element_type=F32)
            s_ref[j] = s
            mx_ref[...] = jnp.maximum(mx_ref[...], fold(s, jnp.maximum))
            return carry

        lax.fori_loop(0, i, scores, 0)
        s = lax.dot_general(qc, k_block(i), dims, preferred_element_type=F32)
        s = jnp.where(key <= tok, s, NEG)
        s_ref[i] = s
        m = jnp.max(jnp.maximum(mx_ref[...], fold(s, jnp.maximum)), axis=-1, keepdims=True)
        ls_ref[...] = jnp.zeros_like(ls_ref)
        acc_ref[...] = jnp.zeros_like(acc_ref)

        def values(j, carry):
            p = jnp.exp(s_ref[j] - m)
            ls_ref[...] += fold(p, jnp.add)
            start = pl.multiple_of(j * tq, tq)
            acc_ref[...] += jnp.dot(p.astype(BF16), v_ref[pl.ds(start, tq), :].astype(BF16),
                                    preferred_element_type=F32)
            return carry

        lax.fori_loop(0, i + 1, values, 0)
        outs.append(acc_ref[...] / jnp.sum(ls_ref[...], axis=-1, keepdims=True))
    lam = _lambda_value(lq1_ref, lk1_ref, lq2_ref, lk2_ref, lam_init)
    o = _sub_norm(outs[0] - lam * outs[1], gs_ref, lam_init)
    for g in range(group):
        o_ref[:, g * 2 * d:(g + 1) * 2 * d] = o[g * tq:(g + 1) * tq].astype(o_ref.dtype)


def _diff_attn_prompt(q, k, qkv, v_col_block, batch, seq, lam_params, g_sub, lam_init):
    d2 = 2 * HEAD_DIM_A
    hw = q.shape[1]
    group = hw // (KV_HEADS_A * d2)
    tq = _tile(seq, 256)
    nqb = seq // tq
    rows = group * tq
    vec = pl.BlockSpec((1, HEAD_DIM_A), lambda b, n, i: (0, 0))
    return pl.pallas_call(
        functools.partial(_diff_prompt_kernel, lam_init, group),
        grid=(batch, KV_HEADS_A, nqb),
        in_specs=[pl.BlockSpec((tq, group * d2), lambda b, n, i: (b * nqb + i, n)),
                  pl.BlockSpec((seq, d2), lambda b, n, i: (b, n)),
                  pl.BlockSpec((seq, d2), lambda b, n, i: (b, v_col_block + n)),
                  vec, vec, vec, vec,
                  pl.BlockSpec((1, d2), lambda b, n, i: (0, 0))],
        out_specs=pl.BlockSpec((tq, group * d2), lambda b, n, i: (b * nqb + i, n)),
        out_shape=jax.ShapeDtypeStruct((batch * seq, hw), BF16),
        scratch_shapes=[pltpu.VMEM((nqb, rows, tq), F32), pltpu.VMEM((rows, LANES), F32),
                        pltpu.VMEM((rows, LANES), F32), pltpu.VMEM((rows, d2), F32)],
        compiler_params=_params("parallel", "parallel", "arbitrary"),
        name="diff_attn_prompt",
    )(q, k, qkv, *lam_params, g_sub.reshape(1, d2))


def _page_rows(a):
    *lead, tokens, n_kv, d2 = a.shape
    a = a.reshape(*lead, tokens, n_kv, 2, d2 // 2)
    return jnp.swapaxes(a, -3, -2).reshape(*lead, tokens * 2 * n_kv, d2 // 2)


def _decode_bias(n_kv, group, t, key_tokens, causal):
    r = jnp.arange(2 * n_kv * group * t)
    c, n, tq = r // (n_kv * group * t), (r // (group * t)) % n_kv, r % t
    k = jnp.arange(key_tokens * 2 * n_kv)
    tk, j, n2 = k // (2 * n_kv), (k // n_kv) % 2, k % n_kv
    ok = (c[:, None] == j[None, :]) & (n[:, None] == n2[None, :])
    if causal:
        ok = ok & (tk[None, :] <= tq[:, None])
    return jnp.where(ok, 0.0, NEG).astype(F32)


def _diff_decode_kernel(lam_init, n_pg, pt_ref, q_ref, *refs):
    del pt_ref
    k_refs, v_refs = refs[:n_pg], refs[n_pg:2 * n_pg]
    (kn_ref, vn_ref, bp_ref, bn_ref, lq1_ref, lk1_ref, lq2_ref, lk2_ref, gs_ref, o_ref,
     m_ref, l_ref, acc_ref) = refs[2 * n_pg:]
    p = pl.program_id(1)
    last = pl.num_programs(1) - 1
    d = HEAD_DIM_A
    half = q_ref.shape[0] // 2
    dims = (((1,), (1,)), ((), ()))

    @pl.when(p == 0)
    def _():
        m_ref[...] = jnp.full_like(m_ref, NEG)
        l_ref[...] = jnp.zeros_like(l_ref)
        acc_ref[...] = jnp.zeros_like(acc_ref)

    q = q_ref[...]

    def update(xk, xv, bias):
        rows = xv.shape[0]
        per_tok = 2 * KV_HEADS_A
        xv_sw = pltpu.roll(xv.reshape(rows // per_tok, per_tok, d), KV_HEADS_A, 1).reshape(rows, d)
        vv = jnp.concatenate([xv.astype(BF16), xv_sw.astype(BF16)], axis=1)
        s = lax.dot_general(q, xk.astype(BF16), dims, preferred_element_type=F32) + bias
        m_old = m_ref[...]
        m_new = jnp.maximum(m_old, jnp.max(s, axis=-1, keepdims=True))
        alpha = jnp.exp(m_old - m_new)
        pr = jnp.exp(s - m_new)
        l_ref[...] = alpha * l_ref[...] + jnp.sum(pr, axis=-1, keepdims=True)
        acc_ref[...] = alpha * acc_ref[...] + jnp.dot(pr.astype(BF16), vv, preferred_element_type=F32)
        m_ref[...] = m_new

    for g in range(n_pg):
        update(k_refs[g][...], v_refs[g][...], bp_ref[...])

    @pl.when(p == last)
    def _():
        update(kn_ref[...], vn_ref[...], bn_ref[...])
        o2 = acc_ref[...] / l_ref[...]
        o1 = o2[:half]
        ob = o2[half:]
        lam = _lambda_value(lq1_ref, lk1_ref, lq2_ref, lk2_ref, lam_init)
        o = o1 - lam * jnp.concatenate([ob[:, d:], ob[:, :d]], axis=1)
        o_ref[...] = _sub_norm(o, gs_ref, lam_init)


def _diff_attn_decode(q_s, cache_k, cache_v, layer, page_table, k_s, v_s, lam_params, g_sub,
                      lam_init):
    nb, t, hw = q_s.shape
    n_pages = page_table.shape[1]
    page = cache_k.shape[2]
    d = HEAD_DIM_A
    d2 = 2 * d
    n_kv = KV_HEADS_A
    group = hw // (n_kv * d2)
    rows = 2 * n_kv * group * t
    page_rows = page * 2 * n_kv
    n_pg = 4 if n_pages % 4 == 0 else (2 if n_pages % 2 == 0 else 1)
    t_pad = -(-t // 16) * 16
    new_rows = t_pad * 2 * n_kv

    q_rows = q_s.reshape(nb, t, n_kv, group, 2, d).transpose(0, 4, 2, 3, 1, 5).reshape(nb, rows, d)

    def new_view(a):
        a = jnp.pad(a.reshape(nb, t, n_kv, d2), ((0, 0), (0, t_pad - t), (0, 0), (0, 0)))
        return _page_rows(a)

    vec = pl.BlockSpec((1, d), lambda b, p, pt: (0, 0))

    def cache_spec(g):
        return pl.BlockSpec((None, None, page_rows, d),
                            lambda b, p, pt: (layer, pt[b * n_pages + p * n_pg + g], 0, 0))

    new_spec = pl.BlockSpec((None, new_rows, d), lambda b, p, pt: (b, 0, 0))
    grid_spec = pltpu.PrefetchScalarGridSpec(
        num_scalar_prefetch=1,
        grid=(nb, n_pages // n_pg),
        in_specs=[pl.BlockSpec((None, rows, d), lambda b, p, pt: (b, 0, 0))]
        + [cache_spec(g) for g in range(n_pg)] * 2
        + [new_spec, new_spec,
           pl.BlockSpec((rows, page_rows), lambda b, p, pt: (0, 0)),
           pl.BlockSpec((rows, new_rows), lambda b, p, pt: (0, 0)),
           vec, vec, vec, vec,
           pl.BlockSpec((1, d2), lambda b, p, pt: (0, 0))],
        out_specs=pl.BlockSpec((None, rows // 2, d2), lambda b, p, pt: (b, 0, 0)),
        scratch_shapes=[pltpu.VMEM((rows, 1), F32), pltpu.VMEM((rows, 1), F32),
                        pltpu.VMEM((rows, d2), F32)],
    )
    ck = _page_rows(cache_k)
    cv = _page_rows(cache_v)
    o = pl.pallas_call(
        functools.partial(_diff_decode_kernel, lam_init, n_pg),
        grid_spec=grid_spec,
        out_shape=jax.ShapeDtypeStruct((nb, rows // 2, d2), F32),
        compiler_params=_params("parallel", "arbitrary"),
        name="diff_attn_decode",
    )(page_table.reshape(-1), q_rows, *([ck] * n_pg), *([cv] * n_pg), new_view(k_s), new_view(v_s),
      _decode_bias(n_kv, group, t, page, False), _decode_bias(n_kv, group, t, t_pad, True),
      *lam_params, g_sub.reshape(1, d2))
    return o.reshape(nb, n_kv, group, t, d2).transpose(0, 3, 1, 2, 4).reshape(nb * t, hw)


def _swa_kernel(group, blocks_per_seq, q_ref, kp_ref, kc_ref, vp_ref, vc_ref, sink_ref, o_ref):
    tq = q_ref.shape[0]
    d = HEAD_DIM_B
    rows = group * tq
    tok = lax.broadcasted_iota(jnp.int32, (rows, WINDOW), 0) % tq
    key = lax.broadcasted_iota(jnp.int32, (rows, WINDOW), 1)
    cur_ok = key <= tok
    if blocks_per_seq is None:
        prev_ok = key > tok
    else:
        first = pl.program_id(0) % blocks_per_seq == 0
        prev_ok = key > tok + jnp.where(first, WINDOW, 0)
    dims = (((1,), (1,)), ((), ()))
    for n in range(KV_HEADS_B):
        cs = slice(n * d, (n + 1) * d)
        qn = jnp.concatenate(
            [q_ref[:, (n * group + g) * d:(n * group + g + 1) * d].astype(BF16) for g in range(group)],
            axis=0)
        sink = jnp.concatenate(
            [jnp.full((tq, 1), sink_ref[n * group + g], F32) for g in range(group)], axis=0)
        sp = lax.dot_general(qn, kp_ref[:, cs].astype(BF16), dims, preferred_element_type=F32)
        sc = lax.dot_general(qn, kc_ref[:, cs].astype(BF16), dims, preferred_element_type=F32)
        sp = jnp.where(prev_ok, sp, NEG)
        sc = jnp.where(cur_ok, sc, NEG)
        m = jnp.maximum(jnp.maximum(jnp.max(sp, axis=-1, keepdims=True),
                                    jnp.max(sc, axis=-1, keepdims=True)), sink)
        pp = jnp.exp(sp - m)
        pc = jnp.exp(sc - m)
        den = (jnp.sum(pp, axis=-1, keepdims=True) + jnp.sum(pc, axis=-1, keepdims=True)
               + jnp.exp(sink - m))
        o = (jnp.dot(pp.astype(BF16), vp_ref[:, cs].astype(BF16), preferred_element_type=F32)
             + jnp.dot(pc.astype(BF16), vc_ref[:, cs].astype(BF16), preferred_element_type=F32)) / den
        for g in range(group):
            col = (n * group + g) * d
            o_ref[:, col:col + d] = o[g * tq:(g + 1) * tq].astype(o_ref.dtype)


def _swa_prompt(q, k_sh, kv, sinks, batch, seq):
    hw = q.shape[1]
    kw = KV_HEADS_B * HEAD_DIM_B
    group = hw // kw
    nb = seq // WINDOW
    qspec = pl.BlockSpec((WINDOW, hw), lambda r: (r, 0))
    return pl.pallas_call(
        functools.partial(_swa_kernel, group, nb),
        grid=(batch * nb,),
        in_specs=[qspec,
                  pl.BlockSpec((WINDOW, kw), lambda r: (jnp.maximum(r - 1, 0), 0)),
                  pl.BlockSpec((WINDOW, kw), lambda r: (r, 0)),
                  pl.BlockSpec((WINDOW, kw), lambda r: (jnp.maximum(r - 1, 0), 1)),
                  pl.BlockSpec((WINDOW, kw), lambda r: (r, 1)),
                  pl.BlockSpec(memory_space=pltpu.SMEM)],
        out_specs=qspec,
        out_shape=jax.ShapeDtypeStruct((batch * seq, hw), BF16),
        compiler_params=_params("parallel"),
        name="swa_attn_prompt",
    )(q, k_sh, k_sh, kv, kv, sinks)


def _swa_sample(q_s, k_prev, k_cur, v_prev, v_cur, sinks):
    nb, t, hw = q_s.shape
    kw = KV_HEADS_B * HEAD_DIM_B
    group = hw // kw
    qspec = pl.BlockSpec((None, t, hw), lambda b: (b, 0, 0))
    kspec = pl.BlockSpec((None, WINDOW, kw), lambda b: (b, 0, 0))
    return pl.pallas_call(
        functools.partial(_swa_kernel, group, None),
        grid=(nb,),
        in_specs=[qspec, kspec, kspec, kspec, kspec, pl.BlockSpec(memory_space=pltpu.SMEM)],
        out_specs=qspec,
        out_shape=jax.ShapeDtypeStruct((nb, t, hw), F32),
        compiler_params=_params("parallel"),
        name="swa_attn_sample",
    )(q_s, k_prev, k_cur, v_prev, v_cur, sinks)


def _row_copy(src_hbm, dst_vmem, sem, src_row, dst_row):
    return pltpu.make_async_copy(src_hbm.at[pl.ds(src_row, 1)], dst_vmem.at[pl.ds(dst_row, 1)], sem)


def _dispatch_kernel(src_ref, cv_ref, h_hbm, o_ref, buf_ref, sem):
    i = pl.program_id(0)
    n = pl.num_programs(0)
    tr = buf_ref.shape[1]
    slot = i % 2

    def start_chunk(chunk, s):
        def body(r, carry):
            _row_copy(h_hbm, buf_ref.at[s], sem.at[s], src_ref[chunk * tr + r], r).start()
            return carry
        lax.fori_loop(0, tr, body, 0)

    def wait_chunk(s):
        def body(r, carry):
            _row_copy(h_hbm, buf_ref.at[s], sem.at[s], 0, r).wait()
            return carry
        lax.fori_loop(0, tr, body, 0)

    @pl.when((i == 0) & (cv_ref[0] == 1))
    def _():
        start_chunk(0, 0)

    nxt = jnp.minimum(i + 1, n - 1)

    @pl.when((i + 1 < n) & (cv_ref[nxt] == 1))
    def _():
        start_chunk(nxt, 1 - slot)

    @pl.when(cv_ref[i] == 1)
    def _():
        wait_chunk(slot)
        o_ref[...] = buf_ref[slot].astype(o_ref.dtype)

    @pl.when(cv_ref[i] == 0)
    def _():
        o_ref[...] = jnp.zeros_like(o_ref)


def _dispatch(h, src, chunk_valid, tr):
    d = h.shape[1]
    p_rows = src.shape[0]
    grid_spec = pltpu.PrefetchScalarGridSpec(
        num_scalar_prefetch=2,
        grid=(p_rows // tr,),
        in_specs=[pl.BlockSpec(memory_space=pl.ANY)],
        out_specs=pl.BlockSpec((tr, d), lambda i, src, cv: (i, 0)),
        scratch_shapes=[pltpu.VMEM((2, tr, d), F32), pltpu.SemaphoreType.DMA((2,))],
    )
    return pl.pallas_call(
        _dispatch_kernel,
        grid_spec=grid_spec,
        out_shape=jax.ShapeDtypeStruct((p_rows, d), BF16),
        compiler_params=_params("arbitrary"),
        name="moe_dispatch",
    )(src, chunk_valid, h)


def _combine_kernel(pos_ref, x_ref, meta_ref, eo_hbm, o_ref, b0_ref, b1_ref, sem):
    i = pl.program_id(0)
    tr = x_ref.shape[0]

    def start(r, carry):
        t = (i * tr + r) * N_TOP
        _row_copy(eo_hbm, b0_ref, sem, pos_ref[t], r).start()
        _row_copy(eo_hbm, b1_ref, sem, pos_ref[t + 1], r).start()
        return carry

    def wait(r, carry):
        _row_copy(eo_hbm, b0_ref, sem, 0, r).wait()
        _row_copy(eo_hbm, b1_ref, sem, 0, r).wait()
        return carry

    lax.fori_loop(0, tr, start, 0)
    lax.fori_loop(0, tr, wait, 0)
    meta = meta_ref[...]
    o_ref[...] = x_ref[...] + meta[:, 2:3] * b0_ref[...] + meta[:, 3:4] * b1_ref[...]


def _combine(x, meta, eo, pos):
    t, d = x.shape
    tr = _tile(t, 256)
    grid_spec = pltpu.PrefetchScalarGridSpec(
        num_scalar_prefetch=1,
        grid=(t // tr,),
        in_specs=[pl.BlockSpec((tr, d), lambda i, pos: (i, 0)),
                  pl.BlockSpec((tr, LANES), lambda i, pos: (i, 0)),
                  pl.BlockSpec(memory_space=pl.ANY)],
        out_specs=pl.BlockSpec((tr, d), lambda i, pos: (i, 0)),
        scratch_shapes=[pltpu.VMEM((tr, d), F32), pltpu.VMEM((tr, d), F32),
                        pltpu.SemaphoreType.DMA(())],
    )
    return pl.pallas_call(
        _combine_kernel,
        grid_spec=grid_spec,
        out_shape=jax.ShapeDtypeStruct((t, d), F32),
        compiler_params=_params("arbitrary"),
        name="moe_combine",
    )(pos, x, meta, eo)


def _moe(x, g_norm, router, w_gu, w_down, layer):
    t, d = x.shape
    n_experts = router.shape[1]
    tm = MOE_ROW_TILE if t >= MOE_ROW_TILE else 32
    n_tiles = -(-(N_TOP * t + n_experts * (tm - 1)) // tm)
    p_rows = n_tiles * tm
    h, meta, counts = _rmsnorm_router(x, g_norm, router)

    cnt = counts[0, :n_experts].astype(jnp.int32)
    tiles_e = (cnt + tm - 1) // tm
    tiles_end = jnp.cumsum(tiles_e)
    row_start = (tiles_end - tiles_e) * tm
    ids = meta[:, 0:2].astype(jnp.int32)
    pos = (row_start[ids] + meta[:, 4:6].astype(jnp.int32)).reshape(-1)
    tok = jnp.repeat(jnp.arange(t, dtype=jnp.int32), N_TOP)
    src = jnp.zeros((p_rows,), jnp.int32).at[pos].set(tok)
    tile_id = jnp.arange(n_tiles, dtype=jnp.int32)
    n_used = tiles_end[-1]
    tv = (tile_id < n_used).astype(jnp.int32)
    last_used = jnp.minimum(tile_id, n_used - 1)
    te = jnp.sum((last_used[:, None] >= tiles_end[None, :]).astype(jnp.int32), axis=1)
    te = jnp.minimum(te, n_experts - 1)
    tr = _tile(tm, 256)
    chunk_valid = jnp.repeat(tv, tm // tr)

    hs = _dispatch(h, src, chunk_valid, tr)
    act = _gu_matmul(hs, w_gu, layer, te, tv, tm)
    eo = _down_matmul(act, w_down, layer, te, tv, tm)
    return _combine(x, meta, eo, pos)


def _dense_ffn(x, g_norm, w_gu, w_down, layer):
    t = x.shape[0]
    tm = _tile(t, 1408)
    ones = jnp.ones((t // tm,), jnp.int32)
    zeros = jnp.zeros((t // tm,), jnp.int32)
    h = _rmsnorm(x, g_norm, BF16)
    act = _gu_matmul(h, w_gu[:, None], layer, zeros, ones, tm)
    return _down_matmul(act, w_down[:, None], layer, zeros, ones, tm, residual=x)


def kernel(x_prompt, x_sample, cache_k, cache_v, cache_swa_k, cache_swa_v, page_table,
           a_norm, a_wqkv, a_qn, a_kn, a_lq1, a_lk1, a_lq2, a_lk2, a_subln, a_wo,
           kv_norm, kv_w, kv_kn, b_norm, b_wq, b_qn, b_sinks, b_wo,
           f_norm, d_wgu, d_wdown, m_router, m_wgu, m_wdown):
    batch, seq, d_model = x_prompt.shape
    dec_b, dec_t, _ = x_sample.shape
    depth = f_norm.shape[0]
    n_a = a_norm.shape[0]
    page = cache_k.shape[2]
    past_len = page_table.shape[1] * page
    t_p = batch * seq
    t_s = dec_b * dec_t
    d2 = 2 * HEAD_DIM_A
    q_a = d_model
    kv_a = KV_HEADS_A * d2
    kw_b = KV_HEADS_B * HEAD_DIM_B
    assert cache_swa_k.shape[1] == WINDOW and seq % WINDOW == 0 and dec_t <= WINDOW <= page

    x = jnp.concatenate([x_prompt.reshape(t_p, d_model), x_sample.reshape(t_s, d_model)], axis=0)
    pos = jnp.concatenate([jnp.tile(jnp.arange(seq), batch),
                           jnp.tile(past_len + jnp.arange(dec_t), dec_b)])
    tab_a = _rope_tables(pos, HEAD_DIM_A)
    tab_b = _rope_tables(pos, HEAD_DIM_B)

    def pad_rows(a, rows):
        return jnp.pad(a, ((0, 0), (0, rows - a.shape[1]), (0, 0)))

    kp_rows, vp_rows, ks_rows, vs_rows = [], [], [], []
    for l in range(depth):
        if l < n_a:
            lam_init = _lambda_init(l)
            lam_params = [v[l].reshape(1, HEAD_DIM_A) for v in (a_lq1, a_lk1, a_lq2, a_lk2)]
            qkv = _matmul(_rmsnorm(x, a_norm[l], BF16), a_wqkv, l)
            q = _head_prep(qkv, 0, q_a, a_qn[l], tab_a, HEAD_DIM_A, HEAD_DIM_A ** -0.5, BF16)
            k = _head_prep(qkv, q_a // kv_a, kv_a, a_kn[l], tab_a, HEAD_DIM_A, 1.0, F32)
            v = qkv[:, q_a + kv_a:]
            o_p = _diff_attn_prompt(q, k, qkv, (q_a + kv_a) // d2, batch, seq, lam_params,
                                    a_subln[l], lam_init)
            k_s = k[t_p:].reshape(dec_b, dec_t, kv_a)
            v_s = v[t_p:].reshape(dec_b, dec_t, kv_a)
            o_s = _diff_attn_decode(q[t_p:].reshape(dec_b, dec_t, q_a), cache_k, cache_v, l,
                                    page_table, k_s, v_s, lam_params, a_subln[l], lam_init)
            o = jnp.concatenate([o_p, o_s.astype(BF16)], axis=0)
            x = _matmul(o, a_wo, l, residual=x)
            kp_rows.append(k[:t_p].reshape(batch, seq, KV_HEADS_A, d2))
            vp_rows.append(v[:t_p].reshape(batch, seq, KV_HEADS_A, d2))
            ks_rows.append(k_s.reshape(dec_b, dec_t, KV_HEADS_A, d2))
            vs_rows.append(v_s.reshape(dec_b, dec_t, KV_HEADS_A, d2))
        else:
            j = l - n_a
            if l == n_a:
                kv = _matmul(_rmsnorm(x, kv_norm, BF16), kv_w[None], 0)
                k_sh = _head_prep(kv, 0, kw_b, kv_kn, tab_b, HEAD_DIM_B, 1.0, F32)
                v_sh = kv[:, kw_b:]
                k_new = k_sh[t_p:].reshape(dec_b, dec_t, kw_b)
                v_new = v_sh[t_p:].reshape(dec_b, dec_t, kw_b)
                k_prev = cache_swa_k.reshape(dec_b, WINDOW, kw_b)
                v_prev = cache_swa_v.reshape(dec_b, WINDOW, kw_b)
            q = _head_prep(_matmul(_rmsnorm(x, b_norm[j], BF16), b_wq, j), 0, d_model, b_qn[j],
                           tab_b, HEAD_DIM_B, HEAD_DIM_B ** -0.5, BF16)
            o_p = _swa_prompt(q, k_sh, kv, b_sinks[j], batch, seq)
            o_s = _swa_sample(q[t_p:].astype(F32).reshape(dec_b, dec_t, d_model), k_prev,
                              pad_rows(k_new, WINDOW), v_prev, pad_rows(v_new, WINDOW), b_sinks[j])
            o = jnp.concatenate([o_p, o_s.reshape(t_s, d_model).astype(BF16)], axis=0)
            x = _matmul(o, b_wo, j, residual=x)
        i = l // 2
        if l % 2 == 0:
            x = _dense_ffn(x, f_norm[l], d_wgu, d_wdown, i)
        else:
            x = _moe(x, f_norm[l], m_router[i], m_wgu, m_wdown, i)

    w_keep = min(WINDOW, seq)
    k_sh_p = k_sh[:t_p].reshape(batch, seq, KV_HEADS_B, HEAD_DIM_B)
    v_sh_p = v_sh[:t_p].reshape(batch, seq, KV_HEADS_B, HEAD_DIM_B)
    swa_k_s = jnp.concatenate([cache_swa_k, k_new.reshape(dec_b, dec_t, KV_HEADS_B, HEAD_DIM_B)],
                              axis=1)[:, dec_t:]
    swa_v_s = jnp.concatenate([cache_swa_v, v_new.reshape(dec_b, dec_t, KV_HEADS_B, HEAD_DIM_B)],
                              axis=1)[:, dec_t:]
    return (x[:t_p].reshape(batch, seq, d_model), x[t_p:].reshape(dec_b, dec_t, d_model),
            jnp.stack(kp_rows), jnp.stack(vp_rows), jnp.stack(ks_rows), jnp.stack(vs_rows),
            k_sh_p[:, seq - w_keep:], v_sh_p[:, seq - w_keep:], swa_k_s, swa_v_s)
```

```python
import functools
import math

import jax
import jax.numpy as jnp
from jax import lax
from jax.experimental import pallas as pl
from jax.experimental.pallas import tpu as pltpu

HEAD_DIM_A = 128
KV_HEADS_A = 4
HEAD_DIM_B = 64
KV_HEADS_B = 8
WINDOW = 128
ROPE_THETA = 500000.0
ROPE_FRACTION = 4
N_TOP = 2
NORM_EPS = 1e-5
LANES = 128
NEG = -1e30
VMEM_LIMIT = 56 * 1024 * 1024
MOE_ROW_TILE = 2304

BF16 = jnp.bfloat16
F32 = jnp.float32


def _lambda_init(layer):
    return 0.8 - 0.6 * math.exp(-0.3 * layer)


def _tile(n, pref, mult=8):
    best = None
    for t in range(mult, min(n, pref) + 1, mult):
        if n % t == 0:
            best = t
    return best if best is not None else n


def _params(*sem):
    return pltpu.CompilerParams(dimension_semantics=sem, vmem_limit_bytes=VMEM_LIMIT)


def _rmsnorm_kernel(x_ref, g_ref, o_ref):
    x = x_ref[...]
    ms = jnp.mean(x * x, axis=-1, keepdims=True)
    o_ref[...] = (x * lax.rsqrt(ms + NORM_EPS) * g_ref[...]).astype(o_ref.dtype)


def _rmsnorm(x, g, out_dtype):
    t, d = x.shape
    tm = _tile(t, 256)
    return pl.pallas_call(
        _rmsnorm_kernel,
        grid=(t // tm,),
        in_specs=[pl.BlockSpec((tm, d), lambda i: (i, 0)),
                  pl.BlockSpec((1, d), lambda i: (0, 0))],
        out_specs=pl.BlockSpec((tm, d), lambda i: (i, 0)),
        out_shape=jax.ShapeDtypeStruct((t, d), out_dtype),
        compiler_params=_params("parallel"),
        name="rmsnorm",
    )(x, g.reshape(1, d))


def _bf16_bits(v):
    u = lax.bitcast_convert_type(v, jnp.uint32)
    return (u + (jnp.uint32(0x7FFF) + ((u >> 16) & jnp.uint32(1)))) >> 16


def _router_kernel(n_experts, x_ref, g_ref, r_ref, h_ref, meta_ref, cnt_ref, carry_ref):
    i = pl.program_id(0)

    @pl.when(i == 0)
    def _():
        carry_ref[...] = jnp.zeros_like(carry_ref)

    x = x_ref[...]
    ms = jnp.mean(x * x, axis=-1, keepdims=True)
    h = x * lax.rsqrt(ms + NORM_EPS) * g_ref[...]
    half = h.shape[1] // 2
    h_ref[...] = _bf16_bits(h[:, :half]) | (_bf16_bits(h[:, half:]) << 16)
    tm = x.shape[0]
    logits = jnp.dot(h, r_ref[...], preferred_element_type=F32, precision=lax.Precision.HIGHEST)
    lane = lax.broadcasted_iota(jnp.int32, (tm, LANES), 1)
    logits = jnp.where(lane < n_experts, logits, NEG)
    m1 = jnp.max(logits, axis=-1, keepdims=True)
    i1 = jnp.min(jnp.where(logits == m1, lane, LANES), axis=-1, keepdims=True)
    rest = jnp.where(lane == i1, NEG, logits)
    m2 = jnp.max(rest, axis=-1, keepdims=True)
    i2 = jnp.min(jnp.where(rest == m2, lane, LANES), axis=-1, keepdims=True)
    e2 = jnp.exp(m2 - m1)
    g1 = 1.0 / (1.0 + e2)
    g2 = e2 / (1.0 + e2)
    sel = jnp.where((lane == i1) | (lane == i2), 1.0, 0.0)
    row = lax.broadcasted_iota(jnp.int32, (tm, tm), 0)
    col = lax.broadcasted_iota(jnp.int32, (tm, tm), 1)
    lower = jnp.where(col < row, 1.0, 0.0).astype(BF16)
    before = jnp.dot(lower, sel.astype(BF16), preferred_element_type=F32)
    rank = carry_ref[...] + before
    r1 = jnp.sum(jnp.where(lane == i1, rank, 0.0), axis=-1, keepdims=True)
    r2 = jnp.sum(jnp.where(lane == i2, rank, 0.0), axis=-1, keepdims=True)
    meta = jnp.where(lane == 0, i1.astype(F32), 0.0)
    meta = jnp.where(lane == 1, i2.astype(F32), meta)
    meta = jnp.where(lane == 2, g1, meta)
    meta = jnp.where(lane == 3, g2, meta)
    meta = jnp.where(lane == 4, r1, meta)
    meta = jnp.where(lane == 5, r2, meta)
    meta_ref[...] = meta
    carry_ref[...] = carry_ref[...] + jnp.sum(sel, axis=0, keepdims=True)
    cnt_ref[...] = carry_ref[...]


def _rmsnorm_router(x, g, router):
    t, d = x.shape
    n_experts = router.shape[1]
    tm = _tile(t, 256)
    r_pad = jnp.pad(router, ((0, 0), (0, LANES - n_experts)))
    return pl.pallas_call(
        functools.partial(_router_kernel, n_experts),
        grid=(t // tm,),
        in_specs=[pl.BlockSpec((tm, d), lambda i: (i, 0)),
                  pl.BlockSpec((1, d), lambda i: (0, 0)),
                  pl.BlockSpec((d, LANES), lambda i: (0, 0))],
        out_specs=[pl.BlockSpec((tm, d // 2), lambda i: (i, 0)),
                   pl.BlockSpec((tm, LANES), lambda i: (i, 0)),
                   pl.BlockSpec((1, LANES), lambda i: (0, 0))],
        out_shape=[jax.ShapeDtypeStruct((t, d // 2), jnp.uint32),
                   jax.ShapeDtypeStruct((t, LANES), F32),
                   jax.ShapeDtypeStruct((1, LANES), F32)],
        scratch_shapes=[pltpu.VMEM((1, LANES), F32)],
        compiler_params=_params("arbitrary"),
        name="rmsnorm_router",
    )(x, g.reshape(1, d), r_pad)


def _mm_kernel(x_ref, w_ref, o_ref):
    o_ref[...] = jnp.dot(x_ref[...], w_ref[...].astype(BF16), preferred_element_type=F32)


def _mm_res_kernel(x_ref, w_ref, r_ref, o_ref):
    o_ref[...] = r_ref[...] + jnp.dot(x_ref[...], w_ref[...].astype(BF16), preferred_element_type=F32)


def _matmul(x, w, layer, residual=None):
    m, k = x.shape
    n = w.shape[2]
    tm = _tile(m, 1408)
    tn = _tile(n, 256, LANES)
    in_specs = [pl.BlockSpec((tm, k), lambda i, j: (i, 0)),
                pl.BlockSpec((None, k, tn), lambda i, j: (layer, 0, j))]
    args = [x, w]
    kern = _mm_kernel
    if residual is not None:
        in_specs.append(pl.BlockSpec((tm, tn), lambda i, j: (i, j)))
        args.append(residual)
        kern = _mm_res_kernel
    return pl.pallas_call(
        kern,
        grid=(m // tm, n // tn),
        in_specs=in_specs,
        out_specs=pl.BlockSpec((tm, tn), lambda i, j: (i, j)),
        out_shape=jax.ShapeDtypeStruct((m, n), F32),
        compiler_params=_params("parallel", "parallel"),
        name="matmul",
    )(*args)


def _gu_kernel(manual_x, te_ref, tv_ref, x_ref, wg_ref, wu_ref, o_ref, *scratch):
    m = pl.program_id(0)
    if manual_x:
        x_buf, sem = scratch
        tm = x_buf.shape[0]

        @pl.when((tv_ref[m] == 1) & (pl.program_id(1) == 0))
        def _():
            rows = pl.ds(pl.multiple_of(m * tm, tm), tm)
            fetch = pltpu.make_async_copy(x_ref.at[rows], x_buf, sem)
            fetch.start()
            fetch.wait()
    else:
        x_buf = x_ref

    @pl.when(tv_ref[m] == 1)
    def _():
        x = x_buf[...]
        g = jnp.dot(x, wg_ref[...].astype(BF16), preferred_element_type=F32)
        u = jnp.dot(x, wu_ref[...].astype(BF16), preferred_element_type=F32)
        o_ref[...] = (g * jax.nn.sigmoid(g) * u).astype(o_ref.dtype)

    @pl.when(tv_ref[m] == 0)
    def _():
        o_ref[...] = jnp.zeros_like(o_ref)


def _gu_matmul(x, w_gu, layer, te, tv, tm):
    m, k = x.shape
    f = w_gu.shape[3] // 2
    tn = _tile(f, 256, LANES)
    nt = f // tn

    def col(j, tv_ref, i):
        return jnp.where(tv_ref[i] == 1, j, nt - 1)

    manual_x = 2 * tm * k * x.dtype.itemsize > VMEM_LIMIT // 2
    if manual_x:
        x_spec = pl.BlockSpec(memory_space=pl.ANY)
        scratch = [pltpu.VMEM((tm, k), x.dtype), pltpu.SemaphoreType.DMA(())]
    else:
        x_spec = pl.BlockSpec((tm, k), lambda i, j, te, tv: (i, 0))
        scratch = []
    grid_spec = pltpu.PrefetchScalarGridSpec(
        num_scalar_prefetch=2,
        grid=(m // tm, nt),
        scratch_shapes=scratch,
        in_specs=[x_spec,
                  pl.BlockSpec((None, None, k, tn),
                               lambda i, j, te, tv: (layer, te[i], 0, col(j, tv, i))),
                  pl.BlockSpec((None, None, k, tn),
                               lambda i, j, te, tv: (layer, te[i], 0, nt + col(j, tv, i)))],
        out_specs=pl.BlockSpec((tm, tn), lambda i, j, te, tv: (i, j)),
    )
    return pl.pallas_call(
        functools.partial(_gu_kernel, manual_x),
        grid_spec=grid_spec,
        out_shape=jax.ShapeDtypeStruct((m, f), BF16),
        compiler_params=_params("arbitrary", "arbitrary"),
        name="swiglu_gate_up",
    )(te, tv, x, w_gu, w_gu)


def _down_kernel(has_res, te_ref, tv_ref, a_ref, w_ref, *rest):
    if has_res:
        r_ref, o_ref, acc_ref = rest
    else:
        o_ref, acc_ref = rest
    m = pl.program_id(0)
    kk = pl.program_id(2)
    last = pl.num_programs(2) - 1

    @pl.when(kk == 0)
    def _():
        acc_ref[...] = jnp.zeros_like(acc_ref)

    @pl.when(tv_ref[m] == 1)
    def _():
        acc_ref[...] += jnp.dot(a_ref[...], w_ref[...].astype(BF16), preferred_element_type=F32)

    @pl.when(kk == last)
    def _():
        if has_res:
            o_ref[...] = r_ref[...] + acc_ref[...]
        else:
            o_ref[...] = acc_ref[...]


def _down_matmul(a, w_down, layer, te, tv, tm, residual=None):
    m, f = a.shape
    n = w_down.shape[3]
    tn = _tile(n, 1024, LANES)
    tk = _tile(f, 1024, LANES)
    nk = f // tk

    def kblk(kk, tv_ref, i):
        return jnp.where(tv_ref[i] == 1, kk, nk - 1)

    in_specs = [pl.BlockSpec((tm, tk), lambda i, j, kk, te, tv: (i, kblk(kk, tv, i))),
                pl.BlockSpec((None, None, tk, tn),
                             lambda i, j, kk, te, tv: (layer, te[i], kblk(kk, tv, i), j))]
    args = [a, w_down]
    if residual is not None:
        in_specs.append(pl.BlockSpec((tm, tn), lambda i, j, kk, te, tv: (i, j)))
        args.append(residual)
    grid_spec = pltpu.PrefetchScalarGridSpec(
        num_scalar_prefetch=2,
        grid=(m // tm, n // tn, nk),
        in_specs=in_specs,
        out_specs=pl.BlockSpec((tm, tn), lambda i, j, kk, te, tv: (i, j)),
        scratch_shapes=[pltpu.VMEM((tm, tn), F32)],
    )
    return pl.pallas_call(
        functools.partial(_down_kernel, residual is not None),
        grid_spec=grid_spec,
        out_shape=jax.ShapeDtypeStruct((m, n), F32),
        compiler_params=_params("parallel", "parallel", "arbitrary"),
        name="swiglu_down",
    )(te, tv, *args)


def _prep_kernel(hd, half, scale, x_ref, g_ref, c_ref, s1_ref, s2_ref, o_ref):
    w = x_ref.shape[1]
    g = g_ref[...]
    c = c_ref[...]
    s1 = s1_ref[...]
    s2 = s2_ref[...]
    lane = lax.broadcasted_iota(jnp.int32, (x_ref.shape[0], LANES), 1)
    for s in range(w // LANES):
        x = x_ref[:, s * LANES:(s + 1) * LANES]
        sq = x * x
        if hd == LANES:
            ms = jnp.mean(sq, axis=-1, keepdims=True)
        else:
            lo = lane < hd
            s_lo = jnp.sum(jnp.where(lo, sq, 0.0), axis=-1, keepdims=True)
            s_hi = jnp.sum(jnp.where(lo, 0.0, sq), axis=-1, keepdims=True)
            ms = jnp.where(lo, s_lo, s_hi) / hd
        y = x * lax.rsqrt(ms + NORM_EPS) * g
        y = y * c + pltpu.roll(y, half, 1) * s1 + pltpu.roll(y, LANES - half, 1) * s2
        if scale != 1.0:
            y = y * scale
        o_ref[:, s * LANES:(s + 1) * LANES] = y.astype(o_ref.dtype)


def _head_prep(x, col_block, width, gain, tables, hd, scale, out_dtype):
    t = x.shape[0]
    tm = _tile(t, 256)
    half = hd // ROPE_FRACTION // 2
    g = jnp.tile(gain, LANES // hd).reshape(1, LANES)
    tab_spec = pl.BlockSpec((tm, LANES), lambda i: (i, 0))
    return pl.pallas_call(
        functools.partial(_prep_kernel, hd, half, scale),
        grid=(t // tm,),
        in_specs=[pl.BlockSpec((tm, width), lambda i: (i, col_block)),
                  pl.BlockSpec((1, LANES), lambda i: (0, 0)),
                  tab_spec, tab_spec, tab_spec],
        out_specs=pl.BlockSpec((tm, width), lambda i: (i, 0)),
        out_shape=jax.ShapeDtypeStruct((t, width), out_dtype),
        compiler_params=_params("parallel"),
        name="head_norm_rope",
    )(x, g, *tables)


def _rope_tables(pos, hd):
    r = hd // ROPE_FRACTION
    half = r // 2
    inv = ROPE_THETA ** (-(2.0 / r) * jnp.arange(half, dtype=F32))
    ang = pos.astype(F32)[:, None] * inv[None, :]
    lane = jnp.arange(LANES) % hd
    cos = jnp.cos(ang)[:, lane % half]
    sin = jnp.sin(ang)[:, lane % half]
    c = jnp.where(lane < r, cos, 1.0)
    s1 = jnp.where((lane >= half) & (lane < r), sin, 0.0)
    s2 = jnp.where(lane < half, -sin, 0.0)
    return c, s1, s2


def _lambda_value(lq1_ref, lk1_ref, lq2_ref, lk2_ref, lam_init):
    a = jnp.sum(lq1_ref[...] * lk1_ref[...], axis=-1, keepdims=True)
    b = jnp.sum(lq2_ref[...] * lk2_ref[...], axis=-1, keepdims=True)
    return jnp.exp(a) - jnp.exp(b) + lam_init


def _sub_norm(o, gs_ref, lam_init):
    ms = jnp.mean(o * o, axis=-1, keepdims=True)
    return o * lax.rsqrt(ms + NORM_EPS) * gs_ref[...] * (1.0 - lam_init)


def _diff_prompt_kernel(lam_init, group, q_ref, k_ref, v_ref, lq1_ref, lk1_ref, lq2_ref, lk2_ref,
                        gs_ref, o_ref, s_ref, mx_ref, ls_ref, acc_ref):
    i = pl.program_id(2)
    tq = q_ref.shape[0]
    d = HEAD_DIM_A
    rows = group * tq
    dims = (((1,), (1,)), ((), ()))
    tok = lax.broadcasted_iota(jnp.int32, (rows, tq), 0) % tq
    key = lax.broadcasted_iota(jnp.int32, (rows, tq), 1)

    def fold(x, op):
        out = x[:, :LANES]
        for t in range(1, tq // LANES):
            out = op(out, x[:, t * LANES:(t + 1) * LANES])
        return out

    outs = []
    for c in range(2):
        qc = jnp.concatenate(
            [q_ref[:, (g * 2 + c) * d:(g * 2 + c + 1) * d] for g in range(group)], axis=0)

        def k_block(j):
            start = pl.multiple_of(j * tq, tq)
            return k_ref[pl.ds(start, tq), c * d:(c + 1) * d].astype(BF16)

        mx_ref[...] = jnp.full_like(mx_ref, NEG)

        def scores(j, carry):
            s = lax.dot_general(qc, k_block(j), dims, preferred_element_type=F32)
            s_ref[j] = s
            mx_ref[...] = jnp.maximum(mx_ref[...], fold(s, jnp.maximum))
            return carry

        lax.fori_loop(0, i, scores, 0)
        s = lax.dot_general(qc, k_block(i), dims, preferred_element_type=F32)
        s = jnp.where(key <= tok, s, NEG)
        s_ref[i] = s
        m = jnp.max(jnp.maximum(mx_ref[...], fold(s, jnp.maximum)), axis=-1, keepdims=True)
        ls_ref[...] = jnp.zeros_like(ls_ref)
        acc_ref[...] = jnp.zeros_like(acc_ref)

        def values(j, carry):
            p = jnp.exp(s_ref[j] - m)
            ls_ref[...] += fold(p, jnp.add)
            start = pl.multiple_of(j * tq, tq)
            acc_ref[...] += jnp.dot(p.astype(BF16), v_ref[pl.ds(start, tq), :].astype(BF16),
                                    preferred_element_type=F32)
            return carry

        lax.fori_loop(0, i + 1, values, 0)
        outs.append(acc_ref[...] / jnp.sum(ls_ref[...], axis=-1, keepdims=True))
    lam = _lambda_value(lq1_ref, lk1_ref, lq2_ref, lk2_ref, lam_init)
    o = _sub_norm(outs[0] - lam * outs[1], gs_ref, lam_init)
    for g in range(group):
        o_ref[:, g * 2 * d:(g + 1) * 2 * d] = o[g * tq:(g + 1) * tq].astype(o_ref.dtype)


def _diff_attn_prompt(q, k, qkv, v_col_block, batch, seq, lam_params, g_sub, lam_init):
    d2 = 2 * HEAD_DIM_A
    hw = q.shape[1]
    group = hw // (KV_HEADS_A * d2)
    tq = _tile(seq, 256)
    nqb = seq // tq
    rows = group * tq
    vec = pl.BlockSpec((1, HEAD_DIM_A), lambda b, n, i: (0, 0))
    return pl.pallas_call(
        functools.partial(_diff_prompt_kernel, lam_init, group),
        grid=(batch, KV_HEADS_A, nqb),
        in_specs=[pl.BlockSpec((tq, group * d2), lambda b, n, i: (b * nqb + i, n)),
                  pl.BlockSpec((seq, d2), lambda b, n, i: (b, n)),
                  pl.BlockSpec((seq, d2), lambda b, n, i: (b, v_col_block + n)),
                  vec, vec, vec, vec,
                  pl.BlockSpec((1, d2), lambda b, n, i: (0, 0))],
        out_specs=pl.BlockSpec((tq, group * d2), lambda b, n, i: (b * nqb + i, n)),
        out_shape=jax.ShapeDtypeStruct((batch * seq, hw), BF16),
        scratch_shapes=[pltpu.VMEM((nqb, rows, tq), F32), pltpu.VMEM((rows, LANES), F32),
                        pltpu.VMEM((rows, LANES), F32), pltpu.VMEM((rows, d2), F32)],
        compiler_params=_params("parallel", "parallel", "arbitrary"),
        name="diff_attn_prompt",
    )(q, k, qkv, *lam_params, g_sub.reshape(1, d2))


def _page_rows(a):
    *lead, tokens, n_kv, d2 = a.shape
    a = a.reshape(*lead, tokens, n_kv, 2, d2 // 2)
    return jnp.swapaxes(a, -3, -2).reshape(*lead, tokens * 2 * n_kv, d2 // 2)


def _decode_bias(n_kv, group, t, key_tokens, causal):
    r = jnp.arange(2 * n_kv * group * t)
    c, n, tq = r // (n_kv * group * t), (r // (group * t)) % n_kv, r % t
    k = jnp.arange(key_tokens * 2 * n_kv)
    tk, j, n2 = k // (2 * n_kv), (k // n_kv) % 2, k % n_kv
    ok = (c[:, None] == j[None, :]) & (n[:, None] == n2[None, :])
    if causal:
        ok = ok & (tk[None, :] <= tq[:, None])
    return jnp.where(ok, 0.0, NEG).astype(F32)


def _diff_decode_kernel(lam_init, n_pg, pt_ref, q_ref, *refs):
    del pt_ref
    k_refs, v_refs = refs[:n_pg], refs[n_pg:2 * n_pg]
    (kn_ref, vn_ref, bp_ref, bn_ref, lq1_ref, lk1_ref, lq2_ref, lk2_ref, gs_ref, o_ref,
     m_ref, l_ref, acc_ref) = refs[2 * n_pg:]
    p = pl.program_id(1)
    last = pl.num_programs(1) - 1
    d = HEAD_DIM_A
    half = q_ref.shape[0] // 2
    dims = (((1,), (1,)), ((), ()))

    @pl.when(p == 0)
    def _():
        m_ref[...] = jnp.full_like(m_ref, NEG)
        l_ref[...] = jnp.zeros_like(l_ref)
        acc_ref[...] = jnp.zeros_like(acc_ref)

    q = q_ref[...]

    def update(xk, xv, bias):
        rows = xv.shape[0]
        per_tok = 2 * KV_HEADS_A
        xv_sw = pltpu.roll(xv.reshape(rows // per_tok, per_tok, d), KV_HEADS_A, 1).reshape(rows, d)
        vv = jnp.concatenate([xv.astype(BF16), xv_sw.astype(BF16)], axis=1)
        s = lax.dot_general(q, xk.astype(BF16), dims, preferred_element_type=F32) + bias
        m_old = m_ref[...]
        m_new = jnp.maximum(m_old, jnp.max(s, axis=-1, keepdims=True))
        alpha = jnp.exp(m_old - m_new)
        pr = jnp.exp(s - m_new)
        l_ref[...] = alpha * l_ref[...] + jnp.sum(pr, axis=-1, keepdims=True)
        acc_ref[...] = alpha * acc_ref[...] + jnp.dot(pr.astype(BF16), vv, preferred_element_type=F32)
        m_ref[...] = m_new

    for g in range(n_pg):
        update(k_refs[g][...], v_refs[g][...], bp_ref[...])

    @pl.when(p == last)
    def _():
        update(kn_ref[...], vn_ref[...], bn_ref[...])
        o2 = acc_ref[...] / l_ref[...]
        o1 = o2[:half]
        ob = o2[half:]
        lam = _lambda_value(lq1_ref, lk1_ref, lq2_ref, lk2_ref, lam_init)
        o = o1 - lam * jnp.concatenate([ob[:, d:], ob[:, :d]], axis=1)
        o_ref[...] = _sub_norm(o, gs_ref, lam_init)


def _diff_attn_decode(q_s, cache_k, cache_v, layer, page_table, k_s, v_s, lam_params, g_sub,
                      lam_init):
    nb, t, hw = q_s.shape
    n_pages = page_table.shape[1]
    page = cache_k.shape[2]
    d = HEAD_DIM_A
    d2 = 2 * d
    n_kv = KV_HEADS_A
    group = hw // (n_kv * d2)
    rows = 2 * n_kv * group * t
    page_rows = page * 2 * n_kv
    n_pg = 4 if n_pages % 4 == 0 else (2 if n_pages % 2 == 0 else 1)
    t_pad = -(-t // 16) * 16
    new_rows = t_pad * 2 * n_kv

    q_rows = q_s.reshape(nb, t, n_kv, group, 2, d).transpose(0, 4, 2, 3, 1, 5).reshape(nb, rows, d)

    def new_view(a):
        a = jnp.pad(a.reshape(nb, t, n_kv, d2), ((0, 0), (0, t_pad - t), (0, 0), (0, 0)))
        return _page_rows(a)

    vec = pl.BlockSpec((1, d), lambda b, p, pt: (0, 0))

    def cache_spec(g):
        return pl.BlockSpec((None, None, page_rows, d),
                            lambda b, p, pt: (layer, pt[b * n_pages + p * n_pg + g], 0, 0))

    new_spec = pl.BlockSpec((None, new_rows, d), lambda b, p, pt: (b, 0, 0))
    grid_spec = pltpu.PrefetchScalarGridSpec(
        num_scalar_prefetch=1,
        grid=(nb, n_pages // n_pg),
        in_specs=[pl.BlockSpec((None, rows, d), lambda b, p, pt: (b, 0, 0))]
        + [cache_spec(g) for g in range(n_pg)] * 2
        + [new_spec, new_spec,
           pl.BlockSpec((rows, page_rows), lambda b, p, pt: (0, 0)),
           pl.BlockSpec((rows, new_rows), lambda b, p, pt: (0, 0)),
           vec, vec, vec, vec,
           pl.BlockSpec((1, d2), lambda b, p, pt: (0, 0))],
        out_specs=pl.BlockSpec((None, rows // 2, d2), lambda b, p, pt: (b, 0, 0)),
        scratch_shapes=[pltpu.VMEM((rows, 1), F32), pltpu.VMEM((rows, 1), F32),
                        pltpu.VMEM((rows, d2), F32)],
    )
    ck = _page_rows(cache_k)
    cv = _page_rows(cache_v)
    o = pl.pallas_call(
        functools.partial(_diff_decode_kernel, lam_init, n_pg),
        grid_spec=grid_spec,
        out_shape=jax.ShapeDtypeStruct((nb, rows // 2, d2), F32),
        compiler_params=_params("parallel", "arbitrary"),
        name="diff_attn_decode",
    )(page_table.reshape(-1), q_rows, *([ck] * n_pg), *([cv] * n_pg), new_view(k_s), new_view(v_s),
      _decode_bias(n_kv, group, t, page, False), _decode_bias(n_kv, group, t, t_pad, True),
      *lam_params, g_sub.reshape(1, d2))
    return o.reshape(nb, n_kv, group, t, d2).transpose(0, 3, 1, 2, 4).reshape(nb * t, hw)


def _swa_kernel(group, blocks_per_seq, q_ref, kp_ref, kc_ref, vp_ref, vc_ref, sink_ref, o_ref):
    tq = q_ref.shape[0]
    d = HEAD_DIM_B
    rows = group * tq
    tok = lax.broadcasted_iota(jnp.int32, (rows, WINDOW), 0) % tq
    key = lax.broadcasted_iota(jnp.int32, (rows, WINDOW), 1)
    cur_ok = key <= tok
    if blocks_per_seq is None:
        prev_ok = key > tok
    else:
        first = pl.program_id(0) % blocks_per_seq == 0
        prev_ok = key > tok + jnp.where(first, WINDOW, 0)
    dims = (((1,), (1,)), ((), ()))
    for n in range(KV_HEADS_B):
        cs = slice(n * d, (n + 1) * d)
        qn = jnp.concatenate(
            [q_ref[:, (n * group + g) * d:(n * group + g + 1) * d].astype(BF16) for g in range(group)],
            axis=0)
        sink = jnp.concatenate(
            [jnp.full((tq, 1), sink_ref[n * group + g], F32) for g in range(group)], axis=0)
        sp = lax.dot_general(qn, kp_ref[:, cs].astype(BF16), dims, preferred_element_type=F32)
        sc = lax.dot_general(qn, kc_ref[:, cs].astype(BF16), dims, preferred_element_type=F32)
        sp = jnp.where(prev_ok, sp, NEG)
        sc = jnp.where(cur_ok, sc, NEG)
        m = jnp.maximum(jnp.max(jnp.maximum(sp, sc), axis=-1, keepdims=True), sink)
        pp = jnp.exp(sp - m)
        pc = jnp.exp(sc - m)
        den = jnp.sum(pp + pc, axis=-1, keepdims=True) + jnp.exp(sink - m)
        o = (jnp.dot(pp.astype(BF16), vp_ref[:, cs].astype(BF16), preferred_element_type=F32)
             + jnp.dot(pc.astype(BF16), vc_ref[:, cs].astype(BF16), preferred_element_type=F32)) / den
        for g in range(group):
            col = (n * group + g) * d
            o_ref[:, col:col + d] = o[g * tq:(g + 1) * tq].astype(o_ref.dtype)


def _swa_prompt(q, k_sh, kv, sinks, batch, seq):
    hw = q.shape[1]
    kw = KV_HEADS_B * HEAD_DIM_B
    group = hw // kw
    nb = seq // WINDOW
    qspec = pl.BlockSpec((WINDOW, hw), lambda r: (r, 0))
    return pl.pallas_call(
        functools.partial(_swa_kernel, group, nb),
        grid=(batch * nb,),
        in_specs=[qspec,
                  pl.BlockSpec((WINDOW, kw), lambda r: (jnp.maximum(r - 1, 0), 0)),
                  pl.BlockSpec((WINDOW, kw), lambda r: (r, 0)),
                  pl.BlockSpec((WINDOW, kw), lambda r: (jnp.maximum(r - 1, 0), 1)),
                  pl.BlockSpec((WINDOW, kw), lambda r: (r, 1)),
                  pl.BlockSpec(memory_space=pltpu.SMEM)],
        out_specs=qspec,
        out_shape=jax.ShapeDtypeStruct((batch * seq, hw), BF16),
        compiler_params=_params("parallel"),
        name="swa_attn_prompt",
    )(q, k_sh, k_sh, kv, kv, sinks)


def _swa_sample(q_s, k_prev, k_cur, v_prev, v_cur, sinks):
    nb, t, hw = q_s.shape
    kw = KV_HEADS_B * HEAD_DIM_B
    group = hw // kw
    qspec = pl.BlockSpec((None, t, hw), lambda b: (b, 0, 0))
    kspec = pl.BlockSpec((None, WINDOW, kw), lambda b: (b, 0, 0))
    return pl.pallas_call(
        functools.partial(_swa_kernel, group, None),
        grid=(nb,),
        in_specs=[qspec, kspec, kspec, kspec, kspec, pl.BlockSpec(memory_space=pltpu.SMEM)],
        out_specs=qspec,
        out_shape=jax.ShapeDtypeStruct((nb, t, hw), F32),
        compiler_params=_params("parallel"),
        name="swa_attn_sample",
    )(q_s, k_prev, k_cur, v_prev, v_cur, sinks)


def _row_copy(src_hbm, dst_vmem, sem, src_row, dst_row):
    return pltpu.make_async_copy(src_hbm.at[pl.ds(src_row, 1)], dst_vmem.at[pl.ds(dst_row, 1)], sem)


def _dispatch_kernel(src_ref, cv_ref, h_hbm, o_ref, buf_ref, sem):
    i = pl.program_id(0)
    tr, half = buf_ref.shape

    @pl.when(cv_ref[i] == 1)
    def _():
        def start(r2, carry):
            for prio in range(2):
                r = 2 * r2 + prio
                _row_copy(h_hbm, buf_ref, sem, src_ref[i * tr + r], r).start(priority=prio)
            return carry

        def wait(r, carry):
            _row_copy(h_hbm, buf_ref, sem, 0, r).wait()
            return carry

        lax.fori_loop(0, tr // 2, start, 0)
        lax.fori_loop(0, tr, wait, 0)
        w = buf_ref[...]
        lo = lax.bitcast_convert_type(w << 16, F32)
        hi = lax.bitcast_convert_type(w & jnp.uint32(0xFFFF0000), F32)
        o_ref[:, :half] = lo.astype(o_ref.dtype)
        o_ref[:, half:] = hi.astype(o_ref.dtype)

    @pl.when(cv_ref[i] == 0)
    def _():
        o_ref[...] = jnp.zeros_like(o_ref)


def _dispatch(h, src, chunk_valid, tr):
    half = h.shape[1]
    d = 2 * half
    p_rows = src.shape[0]
    assert tr % 2 == 0
    grid_spec = pltpu.PrefetchScalarGridSpec(
        num_scalar_prefetch=2,
        grid=(p_rows // tr,),
        in_specs=[pl.BlockSpec(memory_space=pl.ANY)],
        out_specs=pl.BlockSpec((tr, d), lambda i, src, cv: (i, 0)),
        scratch_shapes=[pltpu.VMEM((tr, half), jnp.uint32), pltpu.SemaphoreType.DMA(())],
    )
    return pl.pallas_call(
        _dispatch_kernel,
        grid_spec=grid_spec,
        out_shape=jax.ShapeDtypeStruct((p_rows, d), BF16),
        compiler_params=_params("arbitrary"),
        name="moe_dispatch",
    )(src, chunk_valid, h)


def _combine_kernel(pos_ref, x_ref, meta_ref, eo_hbm, o_ref, b0_ref, b1_ref, sem):
    i = pl.program_id(0)
    tr = x_ref.shape[0]

    def start(r, carry):
        t = (i * tr + r) * N_TOP
        _row_copy(eo_hbm, b0_ref, sem, pos_ref[t], r).start(priority=0)
        _row_copy(eo_hbm, b1_ref, sem, pos_ref[t + 1], r).start(priority=1)
        return carry

    def wait(r, carry):
        _row_copy(eo_hbm, b0_ref, sem, 0, r).wait()
        _row_copy(eo_hbm, b1_ref, sem, 0, r).wait()
        return carry

    lax.fori_loop(0, tr, start, 0)
    lax.fori_loop(0, tr, wait, 0)
    meta = meta_ref[...]
    o_ref[...] = x_ref[...] + meta[:, 2:3] * b0_ref[...] + meta[:, 3:4] * b1_ref[...]


def _combine(x, meta, eo, pos):
    t, d = x.shape
    tr = _tile(t, 256)
    grid_spec = pltpu.PrefetchScalarGridSpec(
        num_scalar_prefetch=1,
        grid=(t // tr,),
        in_specs=[pl.BlockSpec((tr, d), lambda i, pos: (i, 0)),
                  pl.BlockSpec((tr, LANES), lambda i, pos: (i, 0)),
                  pl.BlockSpec(memory_space=pl.ANY)],
        out_specs=pl.BlockSpec((tr, d), lambda i, pos: (i, 0)),
        scratch_shapes=[pltpu.VMEM((tr, d), F32), pltpu.VMEM((tr, d), F32),
                        pltpu.SemaphoreType.DMA(())],
    )
    return pl.pallas_call(
        _combine_kernel,
        grid_spec=grid_spec,
        out_shape=jax.ShapeDtypeStruct((t, d), F32),
        compiler_params=_params("arbitrary"),
        name="moe_combine",
    )(pos, x, meta, eo)


def _moe(x, g_norm, router, w_gu, w_down, layer):
    t, d = x.shape
    n_experts = router.shape[1]
    tm = MOE_ROW_TILE if t >= MOE_ROW_TILE else 32
    n_tiles = -(-(N_TOP * t + n_experts * (tm - 1)) // tm)
    p_rows = n_tiles * tm
    h, meta, counts = _rmsnorm_router(x, g_norm, router)

    cnt = counts[0, :n_experts].astype(jnp.int32)
    tiles_e = (cnt + tm - 1) // tm
    tiles_end = jnp.cumsum(tiles_e)
    row_start = (tiles_end - tiles_e) * tm
    ids = meta[:, 0:2].astype(jnp.int32)
    pos = (row_start[ids] + meta[:, 4:6].astype(jnp.int32)).reshape(-1)
    tok = jnp.repeat(jnp.arange(t, dtype=jnp.int32), N_TOP)
    src = jnp.zeros((p_rows,), jnp.int32).at[pos].set(tok)
    tile_id = jnp.arange(n_tiles, dtype=jnp.int32)
    n_used = tiles_end[-1]
    tv = (tile_id < n_used).astype(jnp.int32)
    last_used = jnp.minimum(tile_id, n_used - 1)
    te = jnp.sum((last_used[:, None] >= tiles_end[None, :]).astype(jnp.int32), axis=1)
    te = jnp.minimum(te, n_experts - 1)
    tr = _tile(tm, 256)
    chunk_valid = jnp.repeat(tv, tm // tr)

    hs = _dispatch(h, src, chunk_valid, tr)
    act = _gu_matmul(hs, w_gu, layer, te, tv, tm)
    eo = _down_matmul(act, w_down, layer, te, tv, tm)
    return _combine(x, meta, eo, pos)


def _dense_ffn(x, g_norm, w_gu, w_down, layer):
    t = x.shape[0]
    tm = _tile(t, 1408)
    ones = jnp.ones((t // tm,), jnp.int32)
    zeros = jnp.zeros((t // tm,), jnp.int32)
    h = _rmsnorm(x, g_norm, BF16)
    act = _gu_matmul(h, w_gu[:, None], layer, zeros, ones, tm)
    return _down_matmul(act, w_down[:, None], layer, zeros, ones, tm, residual=x)


def kernel(x_prompt, x_sample, cache_k, cache_v, cache_swa_k, cache_swa_v, page_table,
           a_norm, a_wqkv, a_qn, a_kn, a_lq1, a_lk1, a_lq2, a_lk2, a_subln, a_wo,
           kv_norm, kv_w, kv_kn, b_norm, b_wq, b_qn, b_sinks, b_wo,
           f_norm, d_wgu, d_wdown, m_router, m_wgu, m_wdown):
    batch, seq, d_model = x_prompt.shape
    dec_b, dec_t, _ = x_sample.shape
    depth = f_norm.shape[0]
    n_a = a_norm.shape[0]
    page = cache_k.shape[2]
    past_len = page_table.shape[1] * page
    t_p = batch * seq
    t_s = dec_b * dec_t
    d2 = 2 * HEAD_DIM_A
    q_a = d_model
    kv_a = KV_HEADS_A * d2
    kw_b = KV_HEADS_B * HEAD_DIM_B
    assert cache_swa_k.shape[1] == WINDOW and seq % WINDOW == 0 and dec_t <= WINDOW <= page

    x = jnp.concatenate([x_prompt.reshape(t_p, d_model), x_sample.reshape(t_s, d_model)], axis=0)
    pos = jnp.concatenate([jnp.tile(jnp.arange(seq), batch),
                           jnp.tile(past_len + jnp.arange(dec_t), dec_b)])
    tab_a = _rope_tables(pos, HEAD_DIM_A)
    tab_b = _rope_tables(pos, HEAD_DIM_B)

    def pad_rows(a, rows):
        return jnp.pad(a, ((0, 0), (0, rows - a.shape[1]), (0, 0)))

    kp_rows, vp_rows, ks_rows, vs_rows = [], [], [], []
    for l in range(depth):
        if l < n_a:
            lam_init = _lambda_init(l)
            lam_params = [v[l].reshape(1, HEAD_DIM_A) for v in (a_lq1, a_lk1, a_lq2, a_lk2)]
            qkv = _matmul(_rmsnorm(x, a_norm[l], BF16), a_wqkv, l)
            q = _head_prep(qkv, 0, q_a, a_qn[l], tab_a, HEAD_DIM_A, HEAD_DIM_A ** -0.5, BF16)
            k = _head_prep(qkv, q_a // kv_a, kv_a, a_kn[l], tab_a, HEAD_DIM_A, 1.0, F32)
            v = qkv[:, q_a + kv_a:]
            o_p = _diff_attn_prompt(q, k, qkv, (q_a + kv_a) // d2, batch, seq, lam_params,
                                    a_subln[l], lam_init)
            k_s = k[t_p:].reshape(dec_b, dec_t, kv_a)
            v_s = v[t_p:].reshape(dec_b, dec_t, kv_a)
            o_s = _diff_attn_decode(q[t_p:].reshape(dec_b, dec_t, q_a), cache_k, cache_v, l,
                                    page_table, k_s, v_s, lam_params, a_subln[l], lam_init)
            o = jnp.concatenate([o_p, o_s.astype(BF16)], axis=0)
            x = _matmul(o, a_wo, l, residual=x)
            kp_rows.append(k[:t_p].reshape(batch, seq, KV_HEADS_A, d2))
            vp_rows.append(v[:t_p].reshape(batch, seq, KV_HEADS_A, d2))
            ks_rows.append(k_s.reshape(dec_b, dec_t, KV_HEADS_A, d2))
            vs_rows.append(v_s.reshape(dec_b, dec_t, KV_HEADS_A, d2))
        else:
            j = l - n_a
            if l == n_a:
                kv = _matmul(_rmsnorm(x, kv_norm, BF16), kv_w[None], 0)
                k_sh = _head_prep(kv, 0, kw_b, kv_kn, tab_b, HEAD_DIM_B, 1.0, F32)
                v_sh = kv[:, kw_b:]
                k_new = k_sh[t_p:].reshape(dec_b, dec_t, kw_b)
                v_new = v_sh[t_p:].reshape(dec_b, dec_t, kw_b)
                k_prev = cache_swa_k.reshape(dec_b, WINDOW, kw_b)
                v_prev = cache_swa_v.reshape(dec_b, WINDOW, kw_b)
            q = _head_prep(_matmul(_rmsnorm(x, b_norm[j], BF16), b_wq, j), 0, d_model, b_qn[j],
                           tab_b, HEAD_DIM_B, HEAD_DIM_B ** -0.5, BF16)
            o_p = _swa_prompt(q, k_sh, kv, b_sinks[j], batch, seq)
            o_s = _swa_sample(q[t_p:].astype(F32).reshape(dec_b, dec_t, d_model), k_prev,
                              pad_rows(k_new, WINDOW), v_prev, pad_rows(v_new, WINDOW), b_sinks[j])
            o = jnp.concatenate([o_p, o_s.reshape(t_s, d_model).astype(BF16)], axis=0)
            x = _matmul(o, b_wo, j, residual=x)
        i = l // 2
        if l % 2 == 0:
            x = _dense_ffn(x, f_norm[l], d_wgu, d_wdown, i)
        else:
            x = _moe(x, f_norm[l], m_router[i], m_wgu, m_wdown, i)

    w_keep = min(WINDOW, seq)
    k_sh_p = k_sh[:t_p].reshape(batch, seq, KV_HEADS_B, HEAD_DIM_B)
    v_sh_p = v_sh[:t_p].reshape(batch, seq, KV_HEADS_B, HEAD_DIM_B)
    swa_k_s = jnp.concatenate([cache_swa_k, k_new.reshape(dec_b, dec_t, KV_HEADS_B, HEAD_DIM_B)],
                              axis=1)[:, dec_t:]
    swa_v_s = jnp.concatenate([cache_swa_v, v_new.reshape(dec_b, dec_t, KV_HEADS_B, HEAD_DIM_B)],
                              axis=1)[:, dec_t:]
    return (x[:t_p].reshape(batch, seq, d_model), x[t_p:].reshape(dec_b, dec_t, d_model),
            jnp.stack(kp_rows), jnp.stack(vp_rows), jnp.stack(ks_rows), jnp.stack(vs_rows),
            k_sh_p[:, seq - w_keep:], v_sh_p[:, seq - w_keep:], swa_k_s, swa_v_s)
```

```python
import functools
import math

import jax
import jax.numpy as jnp
from jax import lax
from jax.experimental import pallas as pl
from jax.experimental.pallas import tpu as pltpu

HEAD_DIM_A = 128
KV_HEADS_A = 4
HEAD_DIM_B = 64
KV_HEADS_B = 8
WINDOW = 128
ROPE_THETA = 500000.0
ROPE_FRACTION = 4
N_TOP = 2
NORM_EPS = 1e-5
LANES = 128
NEG = -1e30
VMEM_LIMIT = 56 * 1024 * 1024
MOE_ROW_TILE = 1152

BF16 = jnp.bfloat16
F32 = jnp.float32


def _lambda_init(layer):
    return 0.8 - 0.6 * math.exp(-0.3 * layer)


def _tile(n, pref, mult=8):
    best = None
    for t in range(mult, min(n, pref) + 1, mult):
        if n % t == 0:
            best = t
    return best if best is not None else n


def _params(*sem):
    return pltpu.CompilerParams(dimension_semantics=sem, vmem_limit_bytes=VMEM_LIMIT)


def _rmsnorm_kernel(x_ref, g_ref, o_ref):
    x = x_ref[...]
    ms = jnp.mean(x * x, axis=-1, keepdims=True)
    o_ref[...] = (x * lax.rsqrt(ms + NORM_EPS) * g_ref[...]).astype(o_ref.dtype)


def _rmsnorm(x, g, out_dtype):
    t, d = x.shape
    tm = _tile(t, 256)
    return pl.pallas_call(
        _rmsnorm_kernel,
        grid=(t // tm,),
        in_specs=[pl.BlockSpec((tm, d), lambda i: (i, 0)),
                  pl.BlockSpec((1, d), lambda i: (0, 0))],
        out_specs=pl.BlockSpec((tm, d), lambda i: (i, 0)),
        out_shape=jax.ShapeDtypeStruct((t, d), out_dtype),
        compiler_params=_params("parallel"),
        name="rmsnorm",
    )(x, g.reshape(1, d))


def _bf16_bits(v):
    u = lax.bitcast_convert_type(v, jnp.uint32)
    return (u + (jnp.uint32(0x7FFF) + ((u >> 16) & jnp.uint32(1)))) >> 16


def _router_kernel(n_experts, x_ref, g_ref, r_ref, h_ref, meta_ref, cnt_ref, carry_ref):
    i = pl.program_id(0)

    @pl.when(i == 0)
    def _():
        carry_ref[...] = jnp.zeros_like(carry_ref)

    x = x_ref[...]
    ms = jnp.mean(x * x, axis=-1, keepdims=True)
    h = x * lax.rsqrt(ms + NORM_EPS) * g_ref[...]
    half = h.shape[1] // 2
    h_ref[...] = _bf16_bits(h[:, :half]) | (_bf16_bits(h[:, half:]) << 16)
    tm = x.shape[0]
    logits = jnp.dot(h, r_ref[...], preferred_element_type=F32, precision=lax.Precision.HIGHEST)
    lane = lax.broadcasted_iota(jnp.int32, (tm, LANES), 1)
    logits = jnp.where(lane < n_experts, logits, NEG)
    m1 = jnp.max(logits, axis=-1, keepdims=True)
    i1 = jnp.min(jnp.where(logits == m1, lane, LANES), axis=-1, keepdims=True)
    rest = jnp.where(lane == i1, NEG, logits)
    m2 = jnp.max(rest, axis=-1, keepdims=True)
    i2 = jnp.min(jnp.where(rest == m2, lane, LANES), axis=-1, keepdims=True)
    e2 = jnp.exp(m2 - m1)
    g1 = 1.0 / (1.0 + e2)
    g2 = e2 / (1.0 + e2)
    sel = jnp.where((lane == i1) | (lane == i2), 1.0, 0.0)
    row = lax.broadcasted_iota(jnp.int32, (tm, tm), 0)
    col = lax.broadcasted_iota(jnp.int32, (tm, tm), 1)
    lower = jnp.where(col < row, 1.0, 0.0).astype(BF16)
    before = jnp.dot(lower, sel.astype(BF16), preferred_element_type=F32)
    rank = carry_ref[...] + before
    r1 = jnp.sum(jnp.where(lane == i1, rank, 0.0), axis=-1, keepdims=True)
    r2 = jnp.sum(jnp.where(lane == i2, rank, 0.0), axis=-1, keepdims=True)
    meta = jnp.where(lane == 0, i1.astype(F32), 0.0)
    meta = jnp.where(lane == 1, i2.astype(F32), meta)
    meta = jnp.where(lane == 2, g1, meta)
    meta = jnp.where(lane == 3, g2, meta)
    meta = jnp.where(lane == 4, r1, meta)
    meta = jnp.where(lane == 5, r2, meta)
    meta_ref[...] = meta
    carry_ref[...] = carry_ref[...] + jnp.sum(sel, axis=0, keepdims=True)
    cnt_ref[...] = carry_ref[...]


def _rmsnorm_router(x, g, router):
    t, d = x.shape
    n_experts = router.shape[1]
    tm = _tile(t, 256)
    r_pad = jnp.pad(router, ((0, 0), (0, LANES - n_experts)))
    return pl.pallas_call(
        functools.partial(_router_kernel, n_experts),
        grid=(t // tm,),
        in_specs=[pl.BlockSpec((tm, d), lambda i: (i, 0)),
                  pl.BlockSpec((1, d), lambda i: (0, 0)),
                  pl.BlockSpec((d, LANES), lambda i: (0, 0))],
        out_specs=[pl.BlockSpec((tm, d // 2), lambda i: (i, 0)),
                   pl.BlockSpec((tm, LANES), lambda i: (i, 0)),
                   pl.BlockSpec((1, LANES), lambda i: (0, 0))],
        out_shape=[jax.ShapeDtypeStruct((t, d // 2), jnp.uint32),
                   jax.ShapeDtypeStruct((t, LANES), F32),
                   jax.ShapeDtypeStruct((1, LANES), F32)],
        scratch_shapes=[pltpu.VMEM((1, LANES), F32)],
        compiler_params=_params("arbitrary"),
        name="rmsnorm_router",
    )(x, g.reshape(1, d), r_pad)


MM_K_CHUNKS = 2


def _mm_kernel(has_res, x_ref, w_ref, *rest):
    o_ref = rest[-1]
    kk = pl.program_id(2)
    tk = w_ref.shape[0]
    for c in range(MM_K_CHUNKS):
        @pl.when(kk == c)
        def _(c=c):
            y = jnp.dot(x_ref[:, c * tk:(c + 1) * tk], w_ref[...].astype(BF16),
                        preferred_element_type=F32)
            if c > 0:
                o_ref[...] += y
            elif has_res:
                o_ref[...] = rest[0][...] + y
            else:
                o_ref[...] = y


def _matmul(x, w, layer, residual=None):
    m, k = x.shape
    n = w.shape[2]
    tm = _tile(m, 1408)
    tn = _tile(n, 512, LANES)
    tk = k // MM_K_CHUNKS
    in_specs = [pl.BlockSpec((tm, k), lambda i, j, kk: (i, 0)),
                pl.BlockSpec((None, tk, tn), lambda i, j, kk: (layer, kk, j))]
    args = [x, w]
    if residual is not None:
        in_specs.append(pl.BlockSpec((tm, tn), lambda i, j, kk: (i, j)))
        args.append(residual)
    return pl.pallas_call(
        functools.partial(_mm_kernel, residual is not None),
        grid=(m // tm, n // tn, MM_K_CHUNKS),
        in_specs=in_specs,
        out_specs=pl.BlockSpec((tm, tn), lambda i, j, kk: (i, j)),
        out_shape=jax.ShapeDtypeStruct((m, n), F32),
        compiler_params=_params("parallel", "parallel", "arbitrary"),
        name="matmul",
    )(*args)


def _gu_kernel(nk, te_ref, tv_ref, x_ref, wg_ref, wu_ref, o_ref, *acc):
    m = pl.program_id(0)
    kk = pl.program_id(2)
    tk = wg_ref.shape[0]

    def finish(g, u):
        o_ref[...] = (g * jax.nn.sigmoid(g) * u).astype(o_ref.dtype)

    for c in range(nk):
        @pl.when((tv_ref[m] == 1) & (kk == c))
        def _(c=c):
            xs = x_ref[:, c * tk:(c + 1) * tk]
            g = jnp.dot(xs, wg_ref[...].astype(BF16), preferred_element_type=F32)
            u = jnp.dot(xs, wu_ref[...].astype(BF16), preferred_element_type=F32)
            if nk == 1:
                finish(g, u)
            elif c == 0:
                acc[0][...] = g
                acc[1][...] = u
            elif c < nk - 1:
                acc[0][...] += g
                acc[1][...] += u
            else:
                finish(acc[0][...] + g, acc[1][...] + u)

    @pl.when(tv_ref[m] == 0)
    def _():
        o_ref[...] = jnp.zeros_like(o_ref)


def _gu_matmul(x, w_gu, layer, te, tv, tm, tn, nk):
    m, k = x.shape
    f = w_gu.shape[3] // 2
    nt = f // tn
    tk = k // nk

    def col(j, tv_ref, i):
        return jnp.where(tv_ref[i] == 1, j, nt - 1)

    def kblk(kk, tv_ref, i):
        return jnp.where(tv_ref[i] == 1, kk, nk - 1)

    grid_spec = pltpu.PrefetchScalarGridSpec(
        num_scalar_prefetch=2,
        grid=(m // tm, nt, nk),
        in_specs=[pl.BlockSpec((tm, k), lambda i, j, kk, te, tv: (i, 0)),
                  pl.BlockSpec((None, None, tk, tn), lambda i, j, kk, te, tv:
                               (layer, te[i], kblk(kk, tv, i), col(j, tv, i))),
                  pl.BlockSpec((None, None, tk, tn), lambda i, j, kk, te, tv:
                               (layer, te[i], kblk(kk, tv, i), nt + col(j, tv, i)))],
        out_specs=pl.BlockSpec((tm, tn), lambda i, j, kk, te, tv: (i, j)),
        scratch_shapes=[pltpu.VMEM((tm, tn), F32)] * (2 if nk > 1 else 0),
    )
    return pl.pallas_call(
        functools.partial(_gu_kernel, nk),
        grid_spec=grid_spec,
        out_shape=jax.ShapeDtypeStruct((m, f), BF16),
        compiler_params=_params("parallel", "arbitrary", "arbitrary"),
        name="swiglu_gate_up",
    )(te, tv, x, w_gu, w_gu)


def _down_kernel(has_res, te_ref, tv_ref, a_ref, w_ref, *rest):
    if has_res:
        r_ref, o_ref, acc_ref = rest
    else:
        o_ref, acc_ref = rest
    m = pl.program_id(0)
    kk = pl.program_id(2)
    last = pl.num_programs(2) - 1

    @pl.when(kk == 0)
    def _():
        acc_ref[...] = jnp.zeros_like(acc_ref)

    @pl.when(tv_ref[m] == 1)
    def _():
        acc_ref[...] += jnp.dot(a_ref[...], w_ref[...].astype(BF16), preferred_element_type=F32)

    @pl.when(kk == last)
    def _():
        if has_res:
            o_ref[...] = r_ref[...] + acc_ref[...]
        else:
            o_ref[...] = acc_ref[...]


def _down_matmul(a, w_down, layer, te, tv, tm, residual=None):
    m, f = a.shape
    n = w_down.shape[3]
    tn = _tile(n, 1024, LANES)
    tk = _tile(f, 1024, LANES)
    nk = f // tk

    def kblk(kk, tv_ref, i):
        return jnp.where(tv_ref[i] == 1, kk, nk - 1)

    in_specs = [pl.BlockSpec((tm, tk), lambda i, j, kk, te, tv: (i, kblk(kk, tv, i))),
                pl.BlockSpec((None, None, tk, tn),
                             lambda i, j, kk, te, tv: (layer, te[i], kblk(kk, tv, i), j))]
    args = [a, w_down]
    if residual is not None:
        in_specs.append(pl.BlockSpec((tm, tn), lambda i, j, kk, te, tv: (i, j)))
        args.append(residual)
    grid_spec = pltpu.PrefetchScalarGridSpec(
        num_scalar_prefetch=2,
        grid=(m // tm, n // tn, nk),
        in_specs=in_specs,
        out_specs=pl.BlockSpec((tm, tn), lambda i, j, kk, te, tv: (i, j)),
        scratch_shapes=[pltpu.VMEM((tm, tn), F32)],
    )
    return pl.pallas_call(
        functools.partial(_down_kernel, residual is not None),
        grid_spec=grid_spec,
        out_shape=jax.ShapeDtypeStruct((m, n), F32),
        compiler_params=_params("parallel", "parallel", "arbitrary"),
        name="swiglu_down",
    )(te, tv, *args)


def _prep_kernel(hd, half, scale, x_ref, g_ref, c_ref, s1_ref, s2_ref, o_ref):
    w = x_ref.shape[1]
    g = g_ref[...]
    c = c_ref[...]
    s1 = s1_ref[...]
    s2 = s2_ref[...]
    lane = lax.broadcasted_iota(jnp.int32, (x_ref.shape[0], LANES), 1)
    for s in range(w // LANES):
        x = x_ref[:, s * LANES:(s + 1) * LANES]
        sq = x * x
        if hd == LANES:
            ms = jnp.mean(sq, axis=-1, keepdims=True)
        else:
            lo = lane < hd
            s_lo = jnp.sum(jnp.where(lo, sq, 0.0), axis=-1, keepdims=True)
            s_hi = jnp.sum(jnp.where(lo, 0.0, sq), axis=-1, keepdims=True)
            ms = jnp.where(lo, s_lo, s_hi) / hd
        y = x * lax.rsqrt(ms + NORM_EPS) * g
        y = y * c + pltpu.roll(y, half, 1) * s1 + pltpu.roll(y, LANES - half, 1) * s2
        if scale != 1.0:
            y = y * scale
        o_ref[:, s * LANES:(s + 1) * LANES] = y.astype(o_ref.dtype)


def _head_prep(x, col_block, width, gain, tables, hd, scale, out_dtype):
    t = x.shape[0]
    tm = _tile(t, 256)
    half = hd // ROPE_FRACTION // 2
    g = jnp.tile(gain, LANES // hd).reshape(1, LANES)
    tab_spec = pl.BlockSpec((tm, LANES), lambda i: (i, 0))
    return pl.pallas_call(
        functools.partial(_prep_kernel, hd, half, scale),
        grid=(t // tm,),
        in_specs=[pl.BlockSpec((tm, width), lambda i: (i, col_block)),
                  pl.BlockSpec((1, LANES), lambda i: (0, 0)),
                  tab_spec, tab_spec, tab_spec],
        out_specs=pl.BlockSpec((tm, width), lambda i: (i, 0)),
        out_shape=jax.ShapeDtypeStruct((t, width), out_dtype),
        compiler_params=_params("parallel"),
        name="head_norm_rope",
    )(x, g, *tables)


def _rope_tables(pos, hd):
    r = hd // ROPE_FRACTION
    half = r // 2
    inv = ROPE_THETA ** (-(2.0 / r) * jnp.arange(half, dtype=F32))
    ang = pos.astype(F32)[:, None] * inv[None, :]
    lane = jnp.arange(LANES) % hd
    cos = jnp.cos(ang)[:, lane % half]
    sin = jnp.sin(ang)[:, lane % half]
    c = jnp.where(lane < r, cos, 1.0)
    s1 = jnp.where((lane >= half) & (lane < r), sin, 0.0)
    s2 = jnp.where(lane < half, -sin, 0.0)
    return c, s1, s2


def _lambda_value(lq1_ref, lk1_ref, lq2_ref, lk2_ref, lam_init):
    a = jnp.sum(lq1_ref[...] * lk1_ref[...], axis=-1, keepdims=True)
    b = jnp.sum(lq2_ref[...] * lk2_ref[...], axis=-1, keepdims=True)
    return jnp.exp(a) - jnp.exp(b) + lam_init


def _sub_norm(o, gs_ref, lam_init):
    ms = jnp.mean(o * o, axis=-1, keepdims=True)
    return o * lax.rsqrt(ms + NORM_EPS) * gs_ref[...] * (1.0 - lam_init)


def _diff_prompt_kernel(lam_init, group, q_ref, k_ref, v_ref, lq1_ref, lk1_ref, lq2_ref, lk2_ref,
                        gs_ref, o_ref, s_ref, mx_ref, ls_ref, acc_ref):
    i = pl.program_id(2)
    tq = q_ref.shape[0]
    d = HEAD_DIM_A
    rows = group * tq
    dims = (((1,), (1,)), ((), ()))
    tok = lax.broadcasted_iota(jnp.int32, (rows, tq), 0) % tq
    key = lax.broadcasted_iota(jnp.int32, (rows, tq), 1)

    def fold(x, op):
        out = x[:, :LANES]
        for t in range(1, tq // LANES):
            out = op(out, x[:, t * LANES:(t + 1) * LANES])
        return out

    outs = []
    for c in range(2):
        qc = jnp.concatenate(
            [q_ref[:, (g * 2 + c) * d:(g * 2 + c + 1) * d] for g in range(group)], axis=0)

        def k_block(j):
            start = pl.multiple_of(j * tq, tq)
            return k_ref[pl.ds(start, tq), c * d:(c + 1) * d].astype(BF16)

        mx_ref[...] = jnp.full_like(mx_ref, NEG)

        def scores(j, carry):
            s = lax.dot_general(qc, k_block(j), dims, preferred_element_type=F32)
            s_ref[j] = s
            mx_ref[...] = jnp.maximum(mx_ref[...], fold(s, jnp.maximum))
            return carry

        lax.fori_loop(0, i, scores, 0)
        s = lax.dot_general(qc, k_block(i), dims, preferred_element_type=F32)
        s = jnp.where(key <= tok, s, NEG)
        s_ref[i] = s
        m = jnp.max(jnp.maximum(mx_ref[...], fold(s, jnp.maximum)), axis=-1, keepdims=True)
        ls_ref[...] = jnp.zeros_like(ls_ref)
        acc_ref[...] = jnp.zeros_like(acc_ref)

        def values(j, carry):
            p = jnp.exp(s_ref[j] - m)
            ls_ref[...] += fold(p, jnp.add)
            start = pl.multiple_of(j * tq, tq)
            acc_ref[...] += jnp.dot(p.astype(BF16), v_ref[pl.ds(start, tq), :].astype(BF16),
                                    preferred_element_type=F32)
            return carry

        lax.fori_loop(0, i + 1, values, 0)
        outs.append(acc_ref[...] / jnp.sum(ls_ref[...], axis=-1, keepdims=True))
    lam = _lambda_value(lq1_ref, lk1_ref, lq2_ref, lk2_ref, lam_init)
    o = _sub_norm(outs[0] - lam * outs[1], gs_ref, lam_init)
    for g in range(group):
        o_ref[:, g * 2 * d:(g + 1) * 2 * d] = o[g * tq:(g + 1) * tq].astype(o_ref.dtype)


def _diff_attn_prompt(q, k, qkv, v_col_block, batch, seq, lam_params, g_sub, lam_init):
    d2 = 2 * HEAD_DIM_A
    hw = q.shape[1]
    group = hw // (KV_HEADS_A * d2)
    tq = _tile(seq, 256)
    nqb = seq // tq
    rows = group * tq
    vec = pl.BlockSpec((1, HEAD_DIM_A), lambda b, n, i: (0, 0))
    return pl.pallas_call(
        functools.partial(_diff_prompt_kernel, lam_init, group),
        grid=(batch, KV_HEADS_A, nqb),
        in_specs=[pl.BlockSpec((tq, group * d2), lambda b, n, i: (b * nqb + i, n)),
                  pl.BlockSpec((seq, d2), lambda b, n, i: (b, n)),
                  pl.BlockSpec((seq, d2), lambda b, n, i: (b, v_col_block + n)),
                  vec, vec, vec, vec,
                  pl.BlockSpec((1, d2), lambda b, n, i: (0, 0))],
        out_specs=pl.BlockSpec((tq, group * d2), lambda b, n, i: (b * nqb + i, n)),
        out_shape=jax.ShapeDtypeStruct((batch * seq, hw), BF16),
        scratch_shapes=[pltpu.VMEM((nqb, rows, tq), F32), pltpu.VMEM((rows, LANES), F32),
                        pltpu.VMEM((rows, LANES), F32), pltpu.VMEM((rows, d2), F32)],
        compiler_params=_params("parallel", "parallel", "arbitrary"),
        name="diff_attn_prompt",
    )(q, k, qkv, *lam_params, g_sub.reshape(1, d2))


def _page_rows(a):
    *lead, tokens, n_kv, d2 = a.shape
    a = a.reshape(*lead, tokens, n_kv, 2, d2 // 2)
    return jnp.swapaxes(a, -3, -2).reshape(*lead, tokens * 2 * n_kv, d2 // 2)


def _decode_bias(n_kv, group, t, key_tokens, causal):
    r = jnp.arange(2 * n_kv * group * t)
    c, n, tq = r // (n_kv * group * t), (r // (group * t)) % n_kv, r % t
    k = jnp.arange(key_tokens * 2 * n_kv)
    tk, j, n2 = k // (2 * n_kv), (k // n_kv) % 2, k % n_kv
    ok = (c[:, None] == j[None, :]) & (n[:, None] == n2[None, :])
    if causal:
        ok = ok & (tk[None, :] <= tq[:, None])
    return jnp.where(ok, 0.0, NEG).astype(F32)


def _diff_decode_kernel(lam_init, n_pg, pt_ref, q_ref, *refs):
    del pt_ref
    k_refs, v_refs = refs[:n_pg], refs[n_pg:2 * n_pg]
    (kn_ref, vn_ref, bp_ref, bn_ref, lq1_ref, lk1_ref, lq2_ref, lk2_ref, gs_ref, o_ref,
     m_ref, l_ref, acc_ref) = refs[2 * n_pg:]
    p = pl.program_id(1)
    last = pl.num_programs(1) - 1
    d = HEAD_DIM_A
    half = q_ref.shape[0] // 2
    dims = (((1,), (1,)), ((), ()))

    @pl.when(p == 0)
    def _():
        m_ref[...] = jnp.full_like(m_ref, NEG)
        l_ref[...] = jnp.zeros_like(l_ref)
        acc_ref[...] = jnp.zeros_like(acc_ref)

    q = q_ref[...]

    def update(xk, xv, bias):
        rows = xv.shape[0]
        per_tok = 2 * KV_HEADS_A
        xv_sw = pltpu.roll(xv.reshape(rows // per_tok, per_tok, d), KV_HEADS_A, 1).reshape(rows, d)
        vv = jnp.concatenate([xv.astype(BF16), xv_sw.astype(BF16)], axis=1)
        s = lax.dot_general(q, xk.astype(BF16), dims, preferred_element_type=F32) + bias
        m_old = m_ref[...]
        m_new = jnp.maximum(m_old, jnp.max(s, axis=-1, keepdims=True))
        alpha = jnp.exp(m_old - m_new)
        pr = jnp.exp(s - m_new)
        l_ref[...] = alpha * l_ref[...] + jnp.sum(pr, axis=-1, keepdims=True)
        acc_ref[...] = alpha * acc_ref[...] + jnp.dot(pr.astype(BF16), vv, preferred_element_type=F32)
        m_ref[...] = m_new

    for g in range(n_pg):
        update(k_refs[g][...], v_refs[g][...], bp_ref[...])

    @pl.when(p == last)
    def _():
        update(kn_ref[...], vn_ref[...], bn_ref[...])
        o2 = acc_ref[...] / l_ref[...]
        o1 = o2[:half]
        ob = o2[half:]
        lam = _lambda_value(lq1_ref, lk1_ref, lq2_ref, lk2_ref, lam_init)
        o = o1 - lam * jnp.concatenate([ob[:, d:], ob[:, :d]], axis=1)
        o_ref[...] = _sub_norm(o, gs_ref, lam_init)


def _diff_attn_decode(q_s, cache_k, cache_v, layer, page_table, k_s, v_s, lam_params, g_sub,
                      lam_init):
    nb, t, hw = q_s.shape
    n_pages = page_table.shape[1]
    page = cache_k.shape[2]
    d = HEAD_DIM_A
    d2 = 2 * d
    n_kv = KV_HEADS_A
    group = hw // (n_kv * d2)
    rows = 2 * n_kv * group * t
    page_rows = page * 2 * n_kv
    n_pg = 4 if n_pages % 4 == 0 else (2 if n_pages % 2 == 0 else 1)
    t_pad = -(-t // 16) * 16
    new_rows = t_pad * 2 * n_kv

    q_rows = q_s.reshape(nb, t, n_kv, group, 2, d).transpose(0, 4, 2, 3, 1, 5).reshape(nb, rows, d)

    def new_view(a):
        a = jnp.pad(a.reshape(nb, t, n_kv, d2), ((0, 0), (0, t_pad - t), (0, 0), (0, 0)))
        return _page_rows(a)

    vec = pl.BlockSpec((1, d), lambda b, p, pt: (0, 0))

    def cache_spec(g):
        return pl.BlockSpec((None, None, page_rows, d),
                            lambda b, p, pt: (layer, pt[b * n_pages + p * n_pg + g], 0, 0))

    new_spec = pl.BlockSpec((None, new_rows, d), lambda b, p, pt: (b, 0, 0))
    grid_spec = pltpu.PrefetchScalarGridSpec(
        num_scalar_prefetch=1,
        grid=(nb, n_pages // n_pg),
        in_specs=[pl.BlockSpec((None, rows, d), lambda b, p, pt: (b, 0, 0))]
        + [cache_spec(g) for g in range(n_pg)] * 2
        + [new_spec, new_spec,
           pl.BlockSpec((rows, page_rows), lambda b, p, pt: (0, 0)),
           pl.BlockSpec((rows, new_rows), lambda b, p, pt: (0, 0)),
           vec, vec, vec, vec,
           pl.BlockSpec((1, d2), lambda b, p, pt: (0, 0))],
        out_specs=pl.BlockSpec((None, rows // 2, d2), lambda b, p, pt: (b, 0, 0)),
        scratch_shapes=[pltpu.VMEM((rows, 1), F32), pltpu.VMEM((rows, 1), F32),
                        pltpu.VMEM((rows, d2), F32)],
    )
    ck = _page_rows(cache_k)
    cv = _page_rows(cache_v)
    o = pl.pallas_call(
        functools.partial(_diff_decode_kernel, lam_init, n_pg),
        grid_spec=grid_spec,
        out_shape=jax.ShapeDtypeStruct((nb, rows // 2, d2), F32),
        compiler_params=_params("parallel", "arbitrary"),
        name="diff_attn_decode",
    )(page_table.reshape(-1), q_rows, *([ck] * n_pg), *([cv] * n_pg), new_view(k_s), new_view(v_s),
      _decode_bias(n_kv, group, t, page, False), _decode_bias(n_kv, group, t, t_pad, True),
      *lam_params, g_sub.reshape(1, d2))
    return o.reshape(nb, n_kv, group, t, d2).transpose(0, 3, 1, 2, 4).reshape(nb * t, hw)


def _swa_kernel(group, blocks_per_seq, q_ref, kp_ref, kc_ref, vp_ref, vc_ref, sink_ref, o_ref):
    tq = q_ref.shape[0]
    d = HEAD_DIM_B
    rows = group * tq
    tok = lax.broadcasted_iota(jnp.int32, (rows, WINDOW), 0) % tq
    key = lax.broadcasted_iota(jnp.int32, (rows, WINDOW), 1)
    cur_ok = key <= tok
    if blocks_per_seq is None:
        prev_ok = key > tok
    else:
        first = pl.program_id(0) % blocks_per_seq == 0
        prev_ok = key > tok + jnp.where(first, WINDOW, 0)
    dims = (((1,), (1,)), ((), ()))
    for n in range(KV_HEADS_B):
        cs = slice(n * d, (n + 1) * d)
        qn = jnp.concatenate(
            [q_ref[:, (n * group + g) * d:(n * group + g + 1) * d].astype(BF16) for g in range(group)],
            axis=0)
        sink = jnp.concatenate(
            [jnp.full((tq, 1), sink_ref[n * group + g], F32) for g in range(group)], axis=0)
        sp = lax.dot_general(qn, kp_ref[:, cs].astype(BF16), dims, preferred_element_type=F32)
        sc = lax.dot_general(qn, kc_ref[:, cs].astype(BF16), dims, preferred_element_type=F32)
        sp = jnp.where(prev_ok, sp, NEG)
        sc = jnp.where(cur_ok, sc, NEG)
        m = jnp.maximum(jnp.max(jnp.maximum(sp, sc), axis=-1, keepdims=True), sink)
        pp = jnp.exp(sp - m)
        pc = jnp.exp(sc - m)
        den = jnp.sum(pp + pc, axis=-1, keepdims=True) + jnp.exp(sink - m)
        o = (jnp.dot(pp.astype(BF16), vp_ref[:, cs].astype(BF16), preferred_element_type=F32)
             + jnp.dot(pc.astype(BF16), vc_ref[:, cs].astype(BF16), preferred_element_type=F32)) / den
        for g in range(group):
            col = (n * group + g) * d
            o_ref[:, col:col + d] = o[g * tq:(g + 1) * tq].astype(o_ref.dtype)


def _swa_prompt(q, k_sh, kv, sinks, batch, seq):
    hw = q.shape[1]
    kw = KV_HEADS_B * HEAD_DIM_B
    group = hw // kw
    nb = seq // WINDOW
    qspec = pl.BlockSpec((WINDOW, hw), lambda r: (r, 0))
    return pl.pallas_call(
        functools.partial(_swa_kernel, group, nb),
        grid=(batch * nb,),
        in_specs=[qspec,
                  pl.BlockSpec((WINDOW, kw), lambda r: (jnp.maximum(r - 1, 0), 0)),
                  pl.BlockSpec((WINDOW, kw), lambda r: (r, 0)),
                  pl.BlockSpec((WINDOW, kw), lambda r: (jnp.maximum(r - 1, 0), 1)),
                  pl.BlockSpec((WINDOW, kw), lambda r: (r, 1)),
                  pl.BlockSpec(memory_space=pltpu.SMEM)],
        out_specs=qspec,
        out_shape=jax.ShapeDtypeStruct((batch * seq, hw), BF16),
        compiler_params=_params("parallel"),
        name="swa_attn_prompt",
    )(q, k_sh, k_sh, kv, kv, sinks)


def _swa_sample(q_s, k_prev, k_cur, v_prev, v_cur, sinks):
    nb, t, hw = q_s.shape
    kw = KV_HEADS_B * HEAD_DIM_B
    group = hw // kw
    qspec = pl.BlockSpec((None, t, hw), lambda b: (b, 0, 0))
    kspec = pl.BlockSpec((None, WINDOW, kw), lambda b: (b, 0, 0))
    return pl.pallas_call(
        functools.partial(_swa_kernel, group, None),
        grid=(nb,),
        in_specs=[qspec, kspec, kspec, kspec, kspec, pl.BlockSpec(memory_space=pltpu.SMEM)],
        out_specs=qspec,
        out_shape=jax.ShapeDtypeStruct((nb, t, hw), F32),
        compiler_params=_params("parallel"),
        name="swa_attn_sample",
    )(q_s, k_prev, k_cur, v_prev, v_cur, sinks)


def _row_copy(src_hbm, dst_vmem, sem, src_row, dst_row):
    return pltpu.make_async_copy(src_hbm.at[pl.ds(src_row, 1)], dst_vmem.at[pl.ds(dst_row, 1)], sem)


def _dispatch_kernel(src_ref, cv_ref, h_hbm, o_ref, buf_ref, sem):
    i = pl.program_id(0)
    tr, half = buf_ref.shape

    @pl.when(cv_ref[i] == 1)
    def _():
        def start(r2, carry):
            for prio in range(2):
                r = 2 * r2 + prio
                _row_copy(h_hbm, buf_ref, sem, src_ref[i * tr + r], r).start(priority=prio)
            return carry

        def wait(r, carry):
            _row_copy(h_hbm, buf_ref, sem, 0, r).wait()
            return carry

        lax.fori_loop(0, tr // 2, start, 0)
        lax.fori_loop(0, tr, wait, 0)
        w = buf_ref[...]
        lo = lax.bitcast_convert_type(w << 16, F32)
        hi = lax.bitcast_convert_type(w & jnp.uint32(0xFFFF0000), F32)
        o_ref[:, :half] = lo.astype(o_ref.dtype)
        o_ref[:, half:] = hi.astype(o_ref.dtype)

    @pl.when(cv_ref[i] == 0)
    def _():
        o_ref[...] = jnp.zeros_like(o_ref)


def _dispatch(h, src, chunk_valid, tr):
    half = h.shape[1]
    d = 2 * half
    p_rows = src.shape[0]
    assert tr % 2 == 0
    grid_spec = pltpu.PrefetchScalarGridSpec(
        num_scalar_prefetch=2,
        grid=(p_rows // tr,),
        in_specs=[pl.BlockSpec(memory_space=pl.ANY)],
        out_specs=pl.BlockSpec((tr, d), lambda i, src, cv: (i, 0)),
        scratch_shapes=[pltpu.VMEM((tr, half), jnp.uint32), pltpu.SemaphoreType.DMA(())],
    )
    return pl.pallas_call(
        _dispatch_kernel,
        grid_spec=grid_spec,
        out_shape=jax.ShapeDtypeStruct((p_rows, d), BF16),
        compiler_params=_params("arbitrary"),
        name="moe_dispatch",
    )(src, chunk_valid, h)


def _combine_kernel(pos_ref, x_ref, meta_ref, eo_hbm, o_ref, b0_ref, b1_ref, sem):
    i = pl.program_id(0)
    tr = x_ref.shape[0]

    def start(r, carry):
        t = (i * tr + r) * N_TOP
        _row_copy(eo_hbm, b0_ref, sem, pos_ref[t], r).start(priority=0)
        _row_copy(eo_hbm, b1_ref, sem, pos_ref[t + 1], r).start(priority=1)
        return carry

    def wait(r, carry):
        _row_copy(eo_hbm, b0_ref, sem, 0, r).wait()
        _row_copy(eo_hbm, b1_ref, sem, 0, r).wait()
        return carry

    lax.fori_loop(0, tr, start, 0)
    lax.fori_loop(0, tr, wait, 0)
    meta = meta_ref[...]
    o_ref[...] = x_ref[...] + meta[:, 2:3] * b0_ref[...] + meta[:, 3:4] * b1_ref[...]


def _combine(x, meta, eo, pos):
    t, d = x.shape
    tr = _tile(t, 256)
    grid_spec = pltpu.PrefetchScalarGridSpec(
        num_scalar_prefetch=1,
        grid=(t // tr,),
        in_specs=[pl.BlockSpec((tr, d), lambda i, pos: (i, 0)),
                  pl.BlockSpec((tr, LANES), lambda i, pos: (i, 0)),
                  pl.BlockSpec(memory_space=pl.ANY)],
        out_specs=pl.BlockSpec((tr, d), lambda i, pos: (i, 0)),
        scratch_shapes=[pltpu.VMEM((tr, d), F32), pltpu.VMEM((tr, d), F32),
                        pltpu.SemaphoreType.DMA(())],
    )
    return pl.pallas_call(
        _combine_kernel,
        grid_spec=grid_spec,
        out_shape=jax.ShapeDtypeStruct((t, d), F32),
        compiler_params=_params("arbitrary"),
        name="moe_combine",
    )(pos, x, meta, eo)


def _moe(x, g_norm, router, w_gu, w_down, layer):
    t, d = x.shape
    n_experts = router.shape[1]
    tm = MOE_ROW_TILE if t >= MOE_ROW_TILE else 32
    n_tiles = -(-(N_TOP * t + n_experts * (tm - 1)) // tm)
    p_rows = n_tiles * tm
    h, meta, counts = _rmsnorm_router(x, g_norm, router)

    cnt = counts[0, :n_experts].astype(jnp.int32)
    tiles_e = (cnt + tm - 1) // tm
    tiles_end = jnp.cumsum(tiles_e)
    row_start = (tiles_end - tiles_e) * tm
    ids = meta[:, 0:2].astype(jnp.int32)
    pos = (row_start[ids] + meta[:, 4:6].astype(jnp.int32)).reshape(-1)
    tok = jnp.repeat(jnp.arange(t, dtype=jnp.int32), N_TOP)
    src = jnp.zeros((p_rows,), jnp.int32).at[pos].set(tok)
    tile_id = jnp.arange(n_tiles, dtype=jnp.int32)
    n_used = tiles_end[-1]
    tv = (tile_id < n_used).astype(jnp.int32)
    last_used = jnp.minimum(tile_id, n_used - 1)
    te = jnp.sum((last_used[:, None] >= tiles_end[None, :]).astype(jnp.int32), axis=1)
    te = jnp.minimum(te, n_experts - 1)
    tr = _tile(tm, 256)
    chunk_valid = jnp.repeat(tv, tm // tr)

    hs = _dispatch(h, src, chunk_valid, tr)
    act = _gu_matmul(hs, w_gu, layer, te, tv, tm, _tile(w_gu.shape[3] // 2, 512, LANES), 2)
    eo = _down_matmul(act, w_down, layer, te, tv, tm)
    return _combine(x, meta, eo, pos)


def _dense_ffn(x, g_norm, w_gu, w_down, layer):
    t = x.shape[0]
    tm = _tile(t, 1408)
    ones = jnp.ones((t // tm,), jnp.int32)
    zeros = jnp.zeros((t // tm,), jnp.int32)
    h = _rmsnorm(x, g_norm, BF16)
    act = _gu_matmul(h, w_gu[:, None], layer, zeros, ones, tm,
                     _tile(w_gu.shape[2] // 2, 256, LANES), 1)
    return _down_matmul(act, w_down[:, None], layer, zeros, ones, tm, residual=x)


def kernel(x_prompt, x_sample, cache_k, cache_v, cache_swa_k, cache_swa_v, page_table,
           a_norm, a_wqkv, a_qn, a_kn, a_lq1, a_lk1, a_lq2, a_lk2, a_subln, a_wo,
           kv_norm, kv_w, kv_kn, b_norm, b_wq, b_qn, b_sinks, b_wo,
           f_norm, d_wgu, d_wdown, m_router, m_wgu, m_wdown):
    batch, seq, d_model = x_prompt.shape
    dec_b, dec_t, _ = x_sample.shape
    depth = f_norm.shape[0]
    n_a = a_norm.shape[0]
    page = cache_k.shape[2]
    past_len = page_table.shape[1] * page
    t_p = batch * seq
    t_s = dec_b * dec_t
    d2 = 2 * HEAD_DIM_A
    q_a = d_model
    kv_a = KV_HEADS_A * d2
    kw_b = KV_HEADS_B * HEAD_DIM_B
    assert cache_swa_k.shape[1] == WINDOW and seq % WINDOW == 0 and dec_t <= WINDOW <= page

    x = jnp.concatenate([x_prompt.reshape(t_p, d_model), x_sample.reshape(t_s, d_model)], axis=0)
    pos = jnp.concatenate([jnp.tile(jnp.arange(seq), batch),
                           jnp.tile(past_len + jnp.arange(dec_t), dec_b)])
    tab_a = _rope_tables(pos, HEAD_DIM_A)
    tab_b = _rope_tables(pos, HEAD_DIM_B)

    def pad_rows(a, rows):
        return jnp.pad(a, ((0, 0), (0, rows - a.shape[1]), (0, 0)))

    kp_rows, vp_rows, ks_rows, vs_rows = [], [], [], []
    for l in range(depth):
        if l < n_a:
            lam_init = _lambda_init(l)
            lam_params = [v[l].reshape(1, HEAD_DIM_A) for v in (a_lq1, a_lk1, a_lq2, a_lk2)]
            qkv = _matmul(_rmsnorm(x, a_norm[l], BF16), a_wqkv, l)
            q = _head_prep(qkv, 0, q_a, a_qn[l], tab_a, HEAD_DIM_A, HEAD_DIM_A ** -0.5, BF16)
            k = _head_prep(qkv, q_a // kv_a, kv_a, a_kn[l], tab_a, HEAD_DIM_A, 1.0, F32)
            v = qkv[:, q_a + kv_a:]
            o_p = _diff_attn_prompt(q, k, qkv, (q_a + kv_a) // d2, batch, seq, lam_params,
                                    a_subln[l], lam_init)
            k_s = k[t_p:].reshape(dec_b, dec_t, kv_a)
            v_s = v[t_p:].reshape(dec_b, dec_t, kv_a)
            o_s = _diff_attn_decode(q[t_p:].reshape(dec_b, dec_t, q_a), cache_k, cache_v, l,
                                    page_table, k_s, v_s, lam_params, a_subln[l], lam_init)
            o = jnp.concatenate([o_p, o_s.astype(BF16)], axis=0)
            x = _matmul(o, a_wo, l, residual=x)
            kp_rows.append(k[:t_p].reshape(batch, seq, KV_HEADS_A, d2))
            vp_rows.append(v[:t_p].reshape(batch, seq, KV_HEADS_A, d2))
            ks_rows.append(k_s.reshape(dec_b, dec_t, KV_HEADS_A, d2))
            vs_rows.append(v_s.reshape(dec_b, dec_t, KV_HEADS_A, d2))
        else:
            j = l - n_a
            if l == n_a:
                kv = _matmul(_rmsnorm(x, kv_norm, BF16), kv_w[None], 0)
                k_sh = _head_prep(kv, 0, kw_b, kv_kn, tab_b, HEAD_DIM_B, 1.0, F32)
                v_sh = kv[:, kw_b:]
                k_new = k_sh[t_p:].reshape(dec_b, dec_t, kw_b)
                v_new = v_sh[t_p:].reshape(dec_b, dec_t, kw_b)
                k_prev = cache_swa_k.reshape(dec_b, WINDOW, kw_b)
                v_prev = cache_swa_v.reshape(dec_b, WINDOW, kw_b)
            q = _head_prep(_matmul(_rmsnorm(x, b_norm[j], BF16), b_wq, j), 0, d_model, b_qn[j],
                           tab_b, HEAD_DIM_B, HEAD_DIM_B ** -0.5, BF16)
            o_p = _swa_prompt(q, k_sh, kv, b_sinks[j], batch, seq)
            o_s = _swa_sample(q[t_p:].astype(F32).reshape(dec_b, dec_t, d_model), k_prev,
                              pad_rows(k_new, WINDOW), v_prev, pad_rows(v_new, WINDOW), b_sinks[j])
            o = jnp.concatenate([o_p, o_s.reshape(t_s, d_model).astype(BF16)], axis=0)
            x = _matmul(o, b_wo, j, residual=x)
        i = l // 2
        if l % 2 == 0:
            x = _dense_ffn(x, f_norm[l], d_wgu, d_wdown, i)
        else:
            x = _moe(x, f_norm[l], m_router[i], m_wgu, m_wdown, i)

    w_keep = min(WINDOW, seq)
    k_sh_p = k_sh[:t_p].reshape(batch, seq, KV_HEADS_B, HEAD_DIM_B)
    v_sh_p = v_sh[:t_p].reshape(batch, seq, KV_HEADS_B, HEAD_DIM_B)
    swa_k_s = jnp.concatenate([cache_swa_k, k_new.reshape(dec_b, dec_t, KV_HEADS_B, HEAD_DIM_B)],
                              axis=1)[:, dec_t:]
    swa_v_s = jnp.concatenate([cache_swa_v, v_new.reshape(dec_b, dec_t, KV_HEADS_B, HEAD_DIM_B)],
                              axis=1)[:, dec_t:]
    return (x[:t_p].reshape(batch, seq, d_model), x[t_p:].reshape(dec_b, dec_t, d_model),
            jnp.stack(kp_rows), jnp.stack(vp_rows), jnp.stack(ks_rows), jnp.stack(vs_rows),
            k_sh_p[:, seq - w_keep:], v_sh_p[:, seq - w_keep:], swa_k_s, swa_v_s)
```
